```python
import jax, jax.numpy as jnp
from jax import lax
import numpy as np

D_MODEL = 1024
BATCH = 4
SEQ = 8192
DEPTH = 1
DEC_BATCH = 32
DEC_SEQ = 4
PAST_LEN = 16384
PAGE_SIZE = 128

A_HEADS = 4
A_DK = 128
A_DV = 128
A_WIDTH = A_HEADS * A_DK
A_CHUNK = 64
ATT_GROUPS = ((128, 1), (512, 4), (2048, 16))
N_GROUPS = 3
G_HEADS = 4
HEAD_DIM = 128
ATT_WIDTH = N_GROUPS * G_HEADS * HEAD_DIM
ATT_BLOCK = 128
ROT_DIM = HEAD_DIM // 4
ROPE_THETA = 500000.0
SPLITS = (A_WIDTH, 2 * A_WIDTH, 3 * A_WIDTH, 4 * A_WIDTH,
          4 * A_WIDTH + ATT_WIDTH, 4 * A_WIDTH + 2 * ATT_WIDTH, 4 * A_WIDTH + 3 * ATT_WIDTH)
IN_COLS = 4 * A_WIDTH + 3 * ATT_WIDTH + 2 * D_MODEL
P_HEADS = 8
N_KEYS = 128
N_EXPERTS = N_KEYS * N_KEYS
P_QDIM = 256
P_HALF = P_QDIM // 2
P_TOPK = 16
P_BLOCK = 256
EPS = 1e-6

kernel_name = "hgrn2_dilated_window_peer_adaln_step"


def rms_norm(x, w):
    xf = x.astype(jnp.float32)
    y = xf * lax.rsqrt(jnp.mean(xf * xf, axis=-1, keepdims=True) + EPS)
    return (y * w.astype(jnp.float32)).astype(x.dtype)


def partial_rope(x, pos):
    inv = ROPE_THETA ** (-jnp.arange(0, ROT_DIM, 2, dtype=jnp.float32) / ROT_DIM)
    ang = pos.astype(jnp.float32)[:, None] * inv[None, :]
    cos, sin = jnp.cos(ang), jnp.sin(ang)
    xr = x[..., :ROT_DIM].astype(jnp.float32)
    x1, x2 = xr[..., :ROT_DIM // 2], xr[..., ROT_DIM // 2:]
    rot = jnp.concatenate([x1 * cos - x2 * sin, x1 * sin + x2 * cos], axis=-1)
    return jnp.concatenate([rot.astype(x.dtype), x[..., ROT_DIM:]], axis=-1)


def hgrn_chunked(q, k, v, logf, S0):
    B, T, H, _ = q.shape
    dv = v.shape[-1]
    C = min(A_CHUNK, T)
    pad = (-T) % C
    N = (T + pad) // C

    def blocks(a):
        a = jnp.pad(a.astype(jnp.float32), ((0, 0), (0, pad), (0, 0), (0, 0)))
        return a.reshape(B, N, C, H, a.shape[-1]).transpose(1, 0, 3, 2, 4)

    qc, kc, vc, gc = blocks(q), blocks(k), blocks(v), blocks(logf)
    causal = jnp.tril(jnp.ones((C, C), dtype=bool))[:, :, None]

    def step(S, inp):
        qb, kb, vb, gb = inp
        b = jnp.cumsum(gb, axis=2)
        decay = jnp.exp(jnp.where(causal, b[:, :, :, None, :] - b[:, :, None, :, :], -jnp.inf))
        A = jnp.einsum('bhtk,bhsk,bhtsk->bhts', qb, kb, decay)
        o = jnp.einsum('bhts,bhsv->bhtv', A, vb) + jnp.einsum('bhtk,bhkv->bhtv', qb * jnp.exp(b), S)
        bl = b[:, :, -1:, :]
        S = jnp.exp(bl[:, :, 0, :])[..., None] * S + jnp.einsum('bhsk,bhsv->bhkv', kb * jnp.exp(bl - b), vb)
        return S, o

    S, o = lax.scan(step, S0.astype(jnp.float32), (qc, kc, vc, gc))
    o = o.transpose(1, 0, 3, 2, 4).reshape(B, N * C, H, dv)[:, :T]
    return o, S


def softmax_rows(s, v, contract):
    m = jnp.max(s, axis=-1, keepdims=True)
    p = jnp.exp(s - m)
    den = jnp.sum(p, axis=-1, keepdims=True)
    o = jnp.einsum(contract, p, v.astype(jnp.float32)) / den
    return o, (m + jnp.log(den))[..., 0]


def dilated_window_prompt(q, k, v, dil, n_back):
    B, H, T, dh = q.shape
    L = T // dil
    nb = -(-L // ATT_BLOCK)
    Lp = nb * ATT_BLOCK

    def stream(a):
        a = a.reshape(B, H, L, dil, dh).transpose(0, 1, 3, 2, 4)
        a = jnp.pad(a, ((0, 0), (0, 0), (0, 0), (0, Lp - L), (0, 0)))
        return a.reshape(B, H, dil, nb, ATT_BLOCK, dh)

    def with_prev(a):
        prev = jnp.pad(a[:, :, :, :-1], ((0, 0), (0, 0), (0, 0), (1, 0), (0, 0), (0, 0)))
        return jnp.concatenate([prev, a], axis=4)

    qs = stream(q)
    k2, v2 = with_prev(stream(k)), with_prev(stream(v))
    s = jnp.einsum('bhrnqd,bhrnkd->bhrnqk', qs, k2, preferred_element_type=jnp.float32) * (HEAD_DIM ** -0.5)
    qi = jnp.arange(ATT_BLOCK)[:, None]
    ki = jnp.arange(2 * ATT_BLOCK)[None, :] - ATT_BLOCK
    rel = qi - ki
    kpos = jnp.arange(nb)[:, None, None] * ATT_BLOCK + ki[None]
    mask = (rel >= 0) & (rel <= n_back) & (kpos >= 0)
    s = jnp.where(mask, s, -jnp.inf)
    o, lse = softmax_rows(s, v2, 'bhrnqk,bhrnkd->bhrnqd')
    o = o.reshape(B, H, dil, Lp, dh)[:, :, :, :L].transpose(0, 1, 3, 2, 4).reshape(B, H, T, dh)
    lse = lse.reshape(B, H, dil, Lp)[..., :L].transpose(0, 1, 3, 2).reshape(B, H, T)
    return o, lse


def dilated_window_sample(q, k_all, v_all, dil, n_back, buf_len):
    T = q.shape[2]
    idx = buf_len + jnp.arange(T)[:, None] - jnp.arange(n_back + 1)[None, :] * dil
    valid = idx >= 0
    idx = jnp.maximum(idx, 0)
    kg, vg = k_all[:, :, idx], v_all[:, :, idx]
    s = jnp.einsum('bhtd,bhtnd->bhtn', q, kg, preferred_element_type=jnp.float32) * (HEAD_DIM ** -0.5)
    s = jnp.where(valid, s, -jnp.inf)
    return softmax_rows(s, vg, 'bhtn,bhtnd->bhtd')


def peer(n, wq, qn_w, k1, k2, u, v):
    B, T, D = n.shape
    flat = n.reshape(-1, D)
    N = flat.shape[0]
    blk = min(P_BLOCK, N)
    pad = (-N) % blk
    flat = jnp.pad(flat, ((0, pad), (0, 0)))

    def one_block(xb):
        q = rms_norm((xb @ wq).reshape(blk, P_HEADS, P_QDIM), qn_w)
        s1 = jnp.einsum('thd,hnd->thn', q[..., :P_HALF], k1, preferred_element_type=jnp.float32)
        s2 = jnp.einsum('thd,hnd->thn', q[..., P_HALF:], k2, preferred_element_type=jnp.float32)
        v1, i1 = lax.top_k(s1, P_TOPK)
        v2, i2 = lax.top_k(s2, P_TOPK)
        cand = (v1[..., :, None] + v2[..., None, :]).reshape(blk, P_HEADS, P_TOPK * P_TOPK)
        cidx = (i1[..., :, None] * N_KEYS + i2[..., None, :]).reshape(blk, P_HEADS, P_TOPK * P_TOPK)
        sc, sel = lax.top_k(cand, P_TOPK)
        eidx = jnp.take_along_axis(cidx, sel, axis=-1)
        g = jax.nn.softmax(sc, axis=-1)
        act = jax.nn.gelu(jnp.einsum('td,thkd->thk', xb, u[eidx], preferred_element_type=jnp.float32))
        return jnp.einsum('thk,thkd->td', (g * act).astype(xb.dtype), v[eidx])

    out = lax.map(one_block, flat.reshape(-1, blk, D)).reshape(-1, D)[:N]
    return out.reshape(B, T, D)


def decoder_layer(x, c, pos, lb, S0, kv_bufs, w_ada, b_ada, norm1_w, norm2_w, w_in, hgrn_norm_w,
                  q_norm_w, k_norm_w, w_branch_a, w_branch_b, w_out, peer_wq, peer_qn_w,
                  peer_k1, peer_k2, peer_u, peer_v):
    B, T, _ = x.shape
    mod = jax.nn.silu(c) @ w_ada + b_ada
    sh1, sc1, g1, sh2, sc2, g2 = jnp.split(mod[:, None, :], 6, axis=-1)
    n1 = rms_norm(x, norm1_w) * (1 + sc1) + sh1
    h = n1 @ w_in
    qa, fa, ia, ga, qb, kb, vb, gl = jnp.split(h, SPLITS, axis=-1)

    f = lb + (1 - lb) * jax.nn.sigmoid(fa.astype(jnp.float32))
    heads = lambda a: a.reshape(B, T, A_HEADS, -1)
    oa, S_new = hgrn_chunked(heads(qa), heads(1 - f), heads(ia), heads(jnp.log(f)), S0)
    oa = rms_norm(oa.astype(x.dtype), hgrn_norm_w) * jax.nn.silu(heads(ga))
    ya = oa.reshape(B, T, A_WIDTH) @ w_branch_a

    att_heads = lambda a: a.reshape(B, T, N_GROUPS * G_HEADS, HEAD_DIM).transpose(0, 2, 1, 3)
    q = partial_rope(rms_norm(att_heads(qb), q_norm_w), pos)
    k = partial_rope(rms_norm(att_heads(kb), k_norm_w), pos)
    v = att_heads(vb)
    outs, lses, new_kv = [], [], []
    for gi, (win, dil) in enumerate(ATT_GROUPS):
        sl = slice(gi * G_HEADS, (gi + 1) * G_HEADS)
        qg, kg, vg = q[:, sl], k[:, sl], v[:, sl]
        n_back = win // dil
        if kv_bufs is None:
            o, lse = dilated_window_prompt(qg, kg, vg, dil, n_back)
            keep = min(win, T)
            new_kv.append(jnp.stack([kg[:, :, T - keep:], vg[:, :, T - keep:]], axis=1))
        else:
            buf = kv_bufs[gi]
            o, lse = dilated_window_sample(qg, jnp.concatenate([buf[:, 0], kg], axis=2),
                                           jnp.concatenate([buf[:, 1], vg], axis=2),
                                           dil, n_back, buf.shape[3])
            new_kv.append(jnp.stack([kg, vg], axis=1))
        outs.append(o)
        lses.append(lse)
    wts = jax.nn.softmax(jnp.stack(lses), axis=0)
    ob = jnp.sum(wts[..., None] * jnp.stack(outs), axis=0)
    yb = ob.transpose(0, 2, 1, 3).reshape(B, T, G_HEADS * HEAD_DIM).astype(x.dtype) @ w_branch_b

    gate_a, gate_b = jnp.split(jax.nn.sigmoid(gl), 2, axis=-1)
    x = x + g1 * ((gate_a * ya + gate_b * yb) @ w_out)

    n2 = rms_norm(x, norm2_w) * (1 + sc2) + sh2
    x = x + g2 * peer(n2, peer_wq, peer_qn_w, peer_k1, peer_k2, peer_u, peer_v)
    return x, new_kv, S_new


def setup_inputs(seed: int = 0) -> dict:
    key = jax.random.key(seed)
    ks = iter(jax.random.split(key, 32))
    nrm = lambda shape, scale: jax.random.normal(next(ks), shape, jnp.float32) * scale
    gain = lambda shape: 1.0 + nrm(shape, 0.02)
    D = D_MODEL
    return {
        "x_prompt": nrm((BATCH, SEQ, D), 1.0),
        "x_sample": nrm((DEC_BATCH, DEC_SEQ, D), 1.0),
        "cache_kv_g1": nrm((DEPTH, DEC_BATCH, 2, G_HEADS, min(ATT_GROUPS[0][0], PAST_LEN), HEAD_DIM), 1.0),
        "cache_kv_g2": nrm((DEPTH, DEC_BATCH, 2, G_HEADS, min(ATT_GROUPS[1][0], PAST_LEN), HEAD_DIM), 1.0),
        "cache_kv_g3": nrm((DEPTH, DEC_BATCH, 2, G_HEADS, min(ATT_GROUPS[2][0], PAST_LEN), HEAD_DIM), 1.0),
        "state_hgrn": nrm((DEPTH, DEC_BATCH, A_HEADS, A_DK, A_DV), 1.0),
        "c_prompt": nrm((BATCH, D), 1.0),
        "c_sample": nrm((DEC_BATCH, D), 1.0),
        "w_ada": nrm((DEPTH, D, 6 * D), 0.5 * D ** -0.5),
        "b_ada": nrm((DEPTH, 6 * D), 0.02),
        "norm1_w": gain((DEPTH, D)),
        "norm2_w": gain((DEPTH, D)),
        "w_in": nrm((DEPTH, D, IN_COLS), D ** -0.5),
        "lb_logits": nrm((DEPTH + 1, A_WIDTH), 0.5),
        "hgrn_norm_w": gain((DEPTH, A_DV)),
        "q_norm_w": gain((DEPTH, HEAD_DIM)),
        "k_norm_w": gain((DEPTH, HEAD_DIM)),
        "w_branch_a": nrm((DEPTH, A_WIDTH, D), A_WIDTH ** -0.5),
        "w_branch_b": nrm((DEPTH, G_HEADS * HEAD_DIM, D), (G_HEADS * HEAD_DIM) ** -0.5),
        "w_out": nrm((DEPTH, D, D), D ** -0.5),
        "peer_wq": nrm((DEPTH, D, P_HEADS * P_QDIM), D ** -0.5),
        "peer_qn_w": gain((DEPTH, P_QDIM)),
        "peer_k1": nrm((DEPTH, P_HEADS, N_KEYS, P_HALF), P_HALF ** -0.5),
        "peer_k2": nrm((DEPTH, P_HEADS, N_KEYS, P_HALF), P_HALF ** -0.5),
        "peer_u": nrm((DEPTH, N_EXPERTS, D), D ** -0.5),
        "peer_v": nrm((DEPTH, N_EXPERTS, D), P_HEADS ** -0.5),
    }


def reference(x_prompt, x_sample, cache_kv_g1, cache_kv_g2, cache_kv_g3, state_hgrn, c_prompt, c_sample,
              w_ada, b_ada, norm1_w, norm2_w, w_in, lb_logits, hgrn_norm_w, q_norm_w, k_norm_w,
              w_branch_a, w_branch_b, w_out, peer_wq, peer_qn_w, peer_k1, peer_k2, peer_u, peer_v):
    pos_p = jnp.arange(x_prompt.shape[1])
    pos_s = PAST_LEN + jnp.arange(x_sample.shape[1])
    lb_all = jnp.cumsum(jax.nn.softmax(lb_logits.astype(jnp.float32), axis=0), axis=0)
    yp, ys = x_prompt, x_sample
    kvp = ([], [], [])
    kvs = ([], [], [])
    Sp_list, Ss_list = [], []
    for l in range(DEPTH):
        wl = (w_ada[l], b_ada[l], norm1_w[l], norm2_w[l], w_in[l], hgrn_norm_w[l], q_norm_w[l], k_norm_w[l],
              w_branch_a[l], w_branch_b[l], w_out[l], peer_wq[l], peer_qn_w[l], peer_k1[l], peer_k2[l],
              peer_u[l], peer_v[l])
        S0p = jnp.zeros((yp.shape[0], A_HEADS, A_DK, A_DV), jnp.float32)
        yp, nkv_p, Sp = decoder_layer(yp, c_prompt, pos_p, lb_all[l], S0p, None, *wl)
        ys, nkv_s, Ss = decoder_layer(ys, c_sample, pos_s, lb_all[l], state_hgrn[l],
                                      (cache_kv_g1[l], cache_kv_g2[l], cache_kv_g3[l]), *wl)
        for gi in range(N_GROUPS):
            kvp[gi].append(nkv_p[gi])
            kvs[gi].append(nkv_s[gi])
        Sp_list.append(Sp.astype(yp.dtype))
        Ss_list.append(Ss.astype(state_hgrn.dtype))
    new_kv_g1_prompt = jnp.stack(kvp[0])
    new_kv_g2_prompt = jnp.stack(kvp[1])
    new_kv_g3_prompt = jnp.stack(kvp[2])
    new_state_hgrn_prompt = jnp.stack(Sp_list)
    new_kv_g1_sample = jnp.stack(kvs[0])
    new_kv_g2_sample = jnp.stack(kvs[1])
    new_kv_g3_sample = jnp.stack(kvs[2])
    new_state_hgrn_sample = jnp.stack(Ss_list)
    return (yp, ys, new_kv_g1_prompt, new_kv_g2_prompt, new_kv_g3_prompt, new_state_hgrn_prompt,
            new_kv_g1_sample, new_kv_g2_sample, new_kv_g3_sample, new_state_hgrn_sample)
```

```python
import functools

import jax
import jax.numpy as jnp
from jax import lax
from jax.experimental import pallas as pl
from jax.experimental.pallas import tpu as pltpu

F32 = jnp.float32
BF16 = jnp.bfloat16

D_MODEL = 1024
PAST_LEN = 16384
A_HEADS = 4
A_DK = 128
A_WIDTH = A_HEADS * A_DK
ATT_GROUPS = ((128, 1), (512, 4), (2048, 16))
N_GROUPS = 3
G_HEADS = 4
HEAD_DIM = 128
ATT_WIDTH = N_GROUPS * G_HEADS * HEAD_DIM
ATT_BLOCK = 128
ROT_DIM = HEAD_DIM // 4
ROPE_THETA = 500000.0
IN_COLS = 4 * A_WIDTH + 3 * ATT_WIDTH + 2 * D_MODEL
P_HEADS = 8
N_KEYS = 128
P_QDIM = 256
P_HALF = P_QDIM // 2
P_TOPK = 16
P_PAIRS = P_HEADS * P_TOPK
EPS = 1e-6

LANES = 128
SUBLANES = 8
HGRN_SUB = 16
ATT_TILE = ATT_BLOCK * ATT_GROUPS[-1][1]
VMEM_LIMIT = 56 * 1024 * 1024

NT_DIMS = (((1,), (1,)), ((), ()))
TN_DIMS = (((0,), (0,)), ((), ()))


def _cparams(*sem):
    return pltpu.CompilerParams(dimension_semantics=sem, vmem_limit_bytes=VMEM_LIMIT)


def _rms(x):
    return x * lax.rsqrt(jnp.mean(x * x, axis=-1, keepdims=True) + EPS)


def _sigmoid(x):
    return 1.0 / (1.0 + jnp.exp(-x))


def _mod_spec(arr, rows_per_group, tm, ngrid):
    if arr.ndim == 3:
        if ngrid == 2:
            return pl.BlockSpec((None, 1, arr.shape[-1]), lambda i, j: (i * tm // rows_per_group, 0, 0))
        return pl.BlockSpec((None, 1, arr.shape[-1]), lambda i: (i * tm // rows_per_group, 0, 0))
    if ngrid == 2:
        return pl.BlockSpec((tm, arr.shape[-1]), lambda i, j: (i, 0))
    return pl.BlockSpec((tm, arr.shape[-1]), lambda i: (i, 0))


def _ada_kernel(c_ref, w_ref, b_ref, o_ref):
    c = c_ref[...]
    s = c * _sigmoid(c)
    o_ref[...] = jnp.dot(s, w_ref[...], preferred_element_type=F32) + b_ref[...]


def _ada(c, w, b):
    rows, d = c.shape
    n = -(-rows // SUBLANES) * SUBLANES
    c = jnp.pad(c, ((0, n - rows), (0, 0)))
    cols = w.shape[1]
    tn = 1024
    out = pl.pallas_call(
        _ada_kernel,
        grid=(cols // tn,),
        in_specs=[pl.BlockSpec((n, d), lambda j: (0, 0)),
                  pl.BlockSpec((d, tn), lambda j: (0, j)),
                  pl.BlockSpec((1, tn), lambda j: (0, j))],
        out_specs=pl.BlockSpec((n, tn), lambda j: (0, j)),
        out_shape=jax.ShapeDtypeStruct((n, cols), F32),
        compiler_params=_cparams("arbitrary"),
        name="ada",
    )(c, w, b.reshape(1, cols))
    return out[:rows]


def _in_proj_kernel(x_ref, sc_ref, sh_ref, nw_ref, w_ref, o_ref, xn_ref):
    @pl.when(pl.program_id(1) == 0)
    def _():
        y = _rms(x_ref[...]) * nw_ref[...]
        y = y * (1.0 + sc_ref[...]) + sh_ref[...]
        xn_ref[...] = y.astype(BF16)

    o_ref[...] = jnp.dot(xn_ref[...], w_ref[...], preferred_element_type=F32)


def _in_proj(x2, sc, sh, nw, w_bf, rows_per_group):
    n, d = x2.shape
    cols = w_bf.shape[1]
    tm = min(1024, n, rows_per_group if sc.ndim == 3 else n)
    tn = 512
    return pl.pallas_call(
        _in_proj_kernel,
        grid=(n // tm, cols // tn),
        in_specs=[pl.BlockSpec((tm, d), lambda i, j: (i, 0)),
                  _mod_spec(sc, rows_per_group, tm, 2),
                  _mod_spec(sh, rows_per_group, tm, 2),
                  pl.BlockSpec((1, d), lambda i, j: (0, 0)),
                  pl.BlockSpec((d, tn), lambda i, j: (0, j))],
        out_specs=pl.BlockSpec((tm, tn), lambda i, j: (i, j)),
        out_shape=jax.ShapeDtypeStruct((n, cols), F32),
        scratch_shapes=[pltpu.VMEM((tm, d), BF16)],
        compiler_params=_cparams("parallel", "arbitrary"),
        name="in_proj",
    )(x2, sc, sh, nw.reshape(1, d), w_bf)


def _hgrn_kernel(q_ref, f_ref, i_ref, g_ref, lb_ref, s0_ref, nw_ref, o_ref, sout_ref,
                 st_ref, b_ref, k_ref, *, tt, sub, t_valid):
    t = pl.program_id(2)

    @pl.when(t == 0)
    def _():
        st_ref[...] = s0_ref[...].T

    lb = lb_ref[...]
    f = lb + (1.0 - lb) * _sigmoid(f_ref[...])
    logf = jnp.log(f)
    kk = 1.0 - f
    if t_valid is not None:
        live = lax.broadcasted_iota(jnp.int32, (tt, 1), 0) < t_valid
        logf = jnp.where(live, logf, 0.0)
        kk = jnp.where(live, kk, 0.0)
    r = lax.broadcasted_iota(jnp.int32, (tt, tt), 0)
    c = lax.broadcasted_iota(jnp.int32, (tt, tt), 1)
    tri = ((r // sub == c // sub) & (c <= r)).astype(F32)
    b_ref[...] = jnp.dot(tri, logf, preferred_element_type=F32, precision=lax.Precision.HIGHEST)
    k_ref[...] = kk

    row = lax.broadcasted_iota(jnp.int32, (sub, 1), 0)
    nw = nw_ref[...]

    def body(i, carry):
        sl = pl.ds(pl.multiple_of(i * sub, sub), sub)
        bq = b_ref[sl, :]
        kq = k_ref[sl, :]
        qq = q_ref[sl, :]
        vv = i_ref[sl, :]
        st = st_ref[...]
        qe = qq * jnp.exp(bq)
        o = lax.dot_general(qe.astype(BF16), st.astype(BF16), NT_DIMS, preferred_element_type=F32)
        for s in range(sub):
            dd = jnp.where(row >= s, bq - bq[s:s + 1, :], -jnp.inf)
            m = qq * kq[s:s + 1, :] * jnp.exp(dd)
            o = o + jnp.sum(m, axis=-1, keepdims=True) * vv[s:s + 1, :]
        bl = bq[sub - 1:sub, :]
        kd = kq * jnp.exp(bl - bq)
        st_ref[...] = st * jnp.exp(bl) + lax.dot_general(vv.astype(BF16), kd.astype(BF16), TN_DIMS,
                                                         preferred_element_type=F32)
        gq = g_ref[sl, :]
        o_ref[sl, :] = _rms(o) * nw * (gq * _sigmoid(gq))
        return carry

    lax.fori_loop(0, tt // sub, body, 0)

    @pl.when(t == pl.num_programs(2) - 1)
    def _():
        sout_ref[...] = st_ref[...].T


def _hgrn(h3, lb, s0, nw, t_valid=None):
    b, t, _ = h3.shape
    tt = min(256, t)
    sub = min(HGRN_SUB, tt)
    blk = lambda off: pl.BlockSpec((None, tt, A_DK), lambda bi, hi, ti: (bi, ti, off + hi))
    return pl.pallas_call(
        functools.partial(_hgrn_kernel, tt=tt, sub=sub, t_valid=t_valid),
        grid=(b, A_HEADS, t // tt),
        in_specs=[blk(0), blk(A_HEADS), blk(2 * A_HEADS), blk(3 * A_HEADS),
                  pl.BlockSpec((None, 1, A_DK), lambda bi, hi, ti: (hi, 0, 0)),
                  pl.BlockSpec((None, None, A_DK, A_DK), lambda bi, hi, ti: (bi, hi, 0, 0)),
                  pl.BlockSpec((1, A_DK), lambda bi, hi, ti: (0, 0))],
        out_specs=[pl.BlockSpec((None, tt, A_DK), lambda bi, hi, ti: (bi, ti, hi)),
                   pl.BlockSpec((None, None, A_DK, A_DK), lambda bi, hi, ti: (bi, hi, 0, 0))],
        out_shape=[jax.ShapeDtypeStruct((b, t, A_WIDTH), F32),
                   jax.ShapeDtypeStruct((b, A_HEADS, A_DK, A_DK), F32)],
        scratch_shapes=[pltpu.VMEM((A_DK, A_DK), F32), pltpu.VMEM((tt, A_DK), F32),
                        pltpu.VMEM((tt, A_DK), F32)],
        compiler_params=_cparams("parallel", "parallel", "arbitrary"),
        name="hgrn",
    )(h3, h3, h3, h3, lb.reshape(A_HEADS, 1, A_DK), s0, nw.reshape(1, A_DK))


def _qk_rope_kernel(h_ref, w_ref, cos_ref, sin_ref, o_ref):
    y = _rms(h_ref[...]) * w_ref[...]
    lane = lax.broadcasted_iota(jnp.int32, y.shape, 1)
    half = ROT_DIM // 2
    partner = jnp.where(lane < half, pltpu.roll(y, LANES - half, axis=1), pltpu.roll(y, half, axis=1))
    o_ref[...] = y * cos_ref[...] + partner * sin_ref[...]


def _rope_tables(pos):
    inv = ROPE_THETA ** (-jnp.arange(0, ROT_DIM, 2, dtype=F32) / ROT_DIM)
    ang = pos.astype(F32)[:, None] * inv[None, :]
    cos, sin = jnp.cos(ang), jnp.sin(ang)
    rest = HEAD_DIM - ROT_DIM
    cos_t = jnp.concatenate([cos, cos, jnp.ones((pos.shape[0], rest), F32)], axis=-1)
    sin_t = jnp.concatenate([-sin, sin, jnp.zeros((pos.shape[0], rest), F32)], axis=-1)
    return cos_t, sin_t


def _qk_rope(h3, qk_w, cos_t, sin_t):
    b, t, _ = h3.shape
    tt = min(1024, t)
    nh = 2 * N_GROUPS * G_HEADS
    off = 4 * A_WIDTH // HEAD_DIM
    return pl.pallas_call(
        _qk_rope_kernel,
        grid=(b, t // tt, nh),
        in_specs=[pl.BlockSpec((None, tt, HEAD_DIM), lambda bi, ti, hi: (bi, ti, off + hi)),
                  pl.BlockSpec((None, 1, HEAD_DIM), lambda bi, ti, hi: (hi // (N_GROUPS * G_HEADS), 0, 0)),
                  pl.BlockSpec((tt, HEAD_DIM), lambda bi, ti, hi: (ti, 0)),
                  pl.BlockSpec((tt, HEAD_DIM), lambda bi, ti, hi: (ti, 0))],
        out_specs=pl.BlockSpec((None, tt, HEAD_DIM), lambda bi, ti, hi: (bi, ti, hi)),
        out_shape=jax.ShapeDtypeStruct((b, t, 2 * ATT_WIDTH), F32),
        compiler_params=_cparams("parallel", "parallel", "arbitrary"),
        name="qk_rope",
    )(h3, qk_w, cos_t, sin_t)


def _softmax_block(s_list, v_list):
    m = s_list[0].max(axis=-1, keepdims=True)
    for s in s_list[1:]:
        m = jnp.maximum(m, s.max(axis=-1, keepdims=True))
    den = None
    o = None
    for s, v in zip(s_list, v_list):
        p = jnp.exp(s - m)
        d = jnp.sum(p, axis=-1, keepdims=True)
        pv = jnp.dot(p.astype(BF16), v.astype(BF16), preferred_element_type=F32)
        den = d if den is None else den + d
        o = pv if o is None else o + pv
    return o / den, m + jnp.log(den)


def _attn_kernel(*refs, tq):
    ins = refs[:15]
    o_ref = refs[15]
    og_refs, lse_refs = refs[16:16 + N_GROUPS], refs[16 + N_GROUPS:16 + 2 * N_GROUPS]
    ti = pl.program_id(2)
    scale = HEAD_DIM ** -0.5
    qi = lax.broadcasted_iota(jnp.int32, (ATT_BLOCK, ATT_BLOCK), 0)
    ki = lax.broadcasted_iota(jnp.int32, (ATT_BLOCK, ATT_BLOCK), 1)
    for g, (win, dil) in enumerate(ATT_GROUPS):
        q_ref, kc_ref, kp_ref, vc_ref, vp_ref = ins[5 * g:5 * g + 5]
        span = ATT_BLOCK * dil
        nblk = tq // span

        def load(ref, start, dil=dil):
            if dil == 1:
                return ref[pl.ds(start, ATT_BLOCK), :]
            return ref[pl.ds(start, ATT_BLOCK, stride=dil), :]

        def body(i, carry, dil=dil, span=span, nblk=nblk, og_ref=og_refs[g], lse_ref=lse_refs[g],
                 q_ref=q_ref, kc_ref=kc_ref, kp_ref=kp_ref, vc_ref=vc_ref, vp_ref=vp_ref, load=load):
            r = i // nblk
            n = i % nblk
            start = r + span * n
            qb = load(q_ref, start).astype(BF16)
            kc = load(kc_ref, start)
            vc = load(vc_ref, start)
            last = r + span * (nblk - 1)
            if nblk == 1:
                kp = load(kp_ref, last)
                vp = load(vp_ref, last)
                has_prev = ti > 0
            else:
                inner = jnp.maximum(start - span, r)
                first = n == 0
                kp = jnp.where(first, load(kp_ref, last), load(kc_ref, inner))
                vp = jnp.where(first, load(vp_ref, last), load(vc_ref, inner))
                has_prev = jnp.logical_or(ti > 0, n > 0)
            s_c = lax.dot_general(qb, kc.astype(BF16), NT_DIMS, preferred_element_type=F32) * scale
            s_p = lax.dot_general(qb, kp.astype(BF16), NT_DIMS, preferred_element_type=F32) * scale
            s_c = jnp.where(ki <= qi, s_c, -jnp.inf)
            s_p = jnp.where(jnp.logical_and(ki >= qi, has_prev), s_p, -jnp.inf)
            o, lse = _softmax_block([s_c, s_p], [vc, vp])
            if dil == 1:
                og_ref[pl.ds(start, ATT_BLOCK), :] = o
                lse_ref[pl.ds(start, ATT_BLOCK), :] = jnp.broadcast_to(lse, o.shape)
            else:
                og_ref[pl.ds(start, ATT_BLOCK, stride=dil), :] = o
                lse_ref[pl.ds(start, ATT_BLOCK, stride=dil), :] = jnp.broadcast_to(lse, o.shape)
            return carry

        lax.fori_loop(0, tq // ATT_BLOCK, body, 0)

    l0, l1, l2 = lse_refs[0][...], lse_refs[1][...], lse_refs[2][...]
    m = jnp.maximum(jnp.maximum(l0, l1), l2)
    e0, e1, e2 = jnp.exp(l0 - m), jnp.exp(l1 - m), jnp.exp(l2 - m)
    z = e0 + e1 + e2
    o_ref[...] = (e0 / z) * og_refs[0][...] + (e1 / z) * og_refs[1][...] + (e2 / z) * og_refs[2][...]


def _attn_prompt(qk3, h3):
    b, t, _ = qk3.shape
    tq = ATT_TILE
    assert t % tq == 0
    nh = N_GROUPS * G_HEADS
    voff = (4 * A_WIDTH + 2 * ATT_WIDTH) // HEAD_DIM
    cur = lambda off: pl.BlockSpec((None, tq, HEAD_DIM), lambda bi, ji, ti, off=off: (bi, ti, off + ji))
    prev = lambda off: pl.BlockSpec((None, tq, HEAD_DIM),
                                    lambda bi, ji, ti, off=off: (bi, jnp.maximum(ti - 1, 0), off + ji))
    in_specs, args = [], []
    for g in range(N_GROUPS):
        in_specs += [cur(g * G_HEADS), cur(nh + g * G_HEADS), prev(nh + g * G_HEADS),
                     cur(voff + g * G_HEADS), prev(voff + g * G_HEADS)]
        args += [qk3, qk3, qk3, h3, h3]
    return pl.pallas_call(
        functools.partial(_attn_kernel, tq=tq),
        grid=(b, G_HEADS, t // tq),
        in_specs=in_specs,
        out_specs=pl.BlockSpec((None, tq, HEAD_DIM), lambda bi, ji, ti: (bi, ti, ji)),
        out_shape=jax.ShapeDtypeStruct((b, t, G_HEADS * HEAD_DIM), F32),
        scratch_shapes=[pltpu.VMEM((tq, HEAD_DIM), F32)] * (2 * N_GROUPS),
        compiler_params=_cparams("parallel", "parallel", "arbitrary"),
        name="attn_prompt",
    )(*args)


def _attn_sample_kernel(*refs, tp, t_valid):
    o_ref = refs[12]
    scale = HEAD_DIM ** -0.5
    outs, lses = [], []
    for g, (win, dil) in enumerate(ATT_GROUPS):
        c_ref, q_ref, k_ref, v_ref = refs[4 * g:4 * g + 4]
        n_back = win // dil
        w = c_ref.shape[1]
        qb = q_ref[...].astype(BF16)
        s_b = lax.dot_general(qb, c_ref[0].astype(BF16), NT_DIMS, preferred_element_type=F32) * scale
        s_n = lax.dot_general(qb, k_ref[...].astype(BF16), NT_DIMS, preferred_element_type=F32) * scale
        tq_b = lax.broadcasted_iota(jnp.int32, (tp, w), 0)
        rb = lax.broadcasted_iota(jnp.int32, (tp, w), 1)
        delta = w + tq_b - rb
        ok_b = jnp.logical_and(delta % dil == 0, delta // dil <= n_back)
        tq_n = lax.broadcasted_iota(jnp.int32, (tp, tp), 0)
        tn = lax.broadcasted_iota(jnp.int32, (tp, tp), 1)
        dn = tq_n - tn
        ok_n = (dn >= 0) & (dn % dil == 0) & (dn // dil <= n_back) & (tn < t_valid)
        s_b = jnp.where(ok_b, s_b, -jnp.inf)
        s_n = jnp.where(ok_n, s_n, -jnp.inf)
        o, lse = _softmax_block([s_b, s_n], [c_ref[1], v_ref[...]])
        outs.append(o)
        lses.append(lse)
    m = jnp.maximum(jnp.maximum(lses[0], lses[1]), lses[2])
    es = [jnp.exp(l - m) for l in lses]
    z = es[0] + es[1] + es[2]
    o_ref[...] = (es[0] / z) * outs[0] + (es[1] / z) * outs[1] + (es[2] / z) * outs[2]


def _attn_sample(qk3, h3, caches, t_valid):
    b, tp, _ = qk3.shape
    nh = N_GROUPS * G_HEADS
    voff = (4 * A_WIDTH + 2 * ATT_WIDTH) // HEAD_DIM
    row = lambda off: pl.BlockSpec((None, tp, HEAD_DIM), lambda bi, ji, off=off: (bi, 0, off + ji))
    in_specs, args = [], []
    for g in range(N_GROUPS):
        w = caches[g].shape[3]
        in_specs += [pl.BlockSpec((None, 2, None, w, HEAD_DIM), lambda bi, ji: (bi, 0, ji, 0, 0)),
                     row(g * G_HEADS), row(nh + g * G_HEADS), row(voff + g * G_HEADS)]
        args += [caches[g], qk3, qk3, h3]
    return pl.pallas_call(
        functools.partial(_attn_sample_kernel, tp=tp, t_valid=t_valid),
        grid=(b, G_HEADS),
        in_specs=in_specs,
        out_specs=pl.BlockSpec((None, tp, HEAD_DIM), lambda bi, ji: (bi, 0, ji)),
        out_shape=jax.ShapeDtypeStruct((b, tp, G_HEADS * HEAD_DIM), F32),
        compiler_params=_cparams("parallel", "arbitrary"),
        name="attn_sample",
    )(*args)


def _out_proj_kernel(oa_ref, ob_ref, ga0_ref, ga1_ref, gb0_ref, gb1_ref, x_ref, g1_ref, sc_ref, sh_ref,
                     nw_ref, wa_ref, wb_ref, wo_ref, x1_ref, n2_ref):
    ya = jnp.dot(oa_ref[...].astype(BF16), wa_ref[...], preferred_element_type=F32)
    yb = jnp.dot(ob_ref[...].astype(BF16), wb_ref[...], preferred_element_type=F32)
    gate_a = _sigmoid(jnp.concatenate([ga0_ref[...], ga1_ref[...]], axis=-1))
    gate_b = _sigmoid(jnp.concatenate([gb0_ref[...], gb1_ref[...]], axis=-1))
    mix = jnp.dot((gate_a * ya + gate_b * yb).astype(BF16), wo_ref[...], preferred_element_type=F32)
    x1 = x_ref[...] + g1_ref[...] * mix
    x1_ref[...] = x1
    n2_ref[...] = (_rms(x1) * nw_ref[...]) * (1.0 + sc_ref[...]) + sh_ref[...]


def _out_proj(oa2, ob2, h2, x2, g1, sc2, sh2, nw2, wa_bf, wb_bf, wo_bf, rows_per_group):
    n, d = x2.shape
    tm = min(512, n, rows_per_group if g1.ndim == 3 else n)
    gw = 512
    goff = (4 * A_WIDTH + 3 * ATT_WIDTH) // gw
    rows = lambda width: pl.BlockSpec((tm, width), lambda i: (i, 0))
    gcol = lambda k: pl.BlockSpec((tm, gw), lambda i, k=k: (i, goff + k))
    full = lambda a: pl.BlockSpec(a.shape, lambda i: (0, 0))
    mod = lambda a: _mod_spec(a, rows_per_group, tm, 1)
    return pl.pallas_call(
        _out_proj_kernel,
        grid=(n // tm,),
        in_specs=[rows(A_WIDTH), rows(G_HEADS * HEAD_DIM), gcol(0), gcol(1), gcol(2), gcol(3), rows(d),
                  mod(g1), mod(sc2), mod(sh2), pl.BlockSpec((1, d), lambda i: (0, 0)),
                  full(wa_bf), full(wb_bf), full(wo_bf)],
        out_specs=[rows(d), rows(d)],
        out_shape=[jax.ShapeDtypeStruct((n, d), F32), jax.ShapeDtypeStruct((n, d), F32)],
        compiler_params=_cparams("parallel"),
        name="out_proj",
    )(oa2, ob2, h2, h2, h2, h2, x2, g1, sc2, sh2, nw2.reshape(1, d), wa_bf, wb_bf, wo_bf)


def _topk_rows(s, k, extra=None):
    rows = s.shape[0]
    ridx = lax.broadcasted_iota(jnp.int32, s.shape, 0)
    vals, idxs, extras = [], [], []
    for _ in range(k):
        m = jnp.max(s, axis=0, keepdims=True)
        idx = jnp.min(jnp.where(s == m, ridx, rows), axis=0, keepdims=True)
        hit = ridx == idx
        vals.append(m)
        idxs.append(idx)
        if extra is not None:
            extras.append(jnp.sum(jnp.where(hit, extra, 0), axis=0, keepdims=True))
        s = jnp.where(hit, -jnp.inf, s)
    return vals, idxs, extras


def _peer_sel_kernel(n2_ref, wq_ref, qn_ref, k1_ref, k2_ref, e_ref, g_ref):
    q = jnp.dot(n2_ref[...].astype(BF16), wq_ref[...], preferred_element_type=F32)
    q = _rms(q) * qn_ref[...]
    hi = lax.Precision.HIGHEST
    s1 = lax.dot_general(k1_ref[...], q[:, :P_HALF], NT_DIMS, preferred_element_type=F32, precision=hi)
    s2 = lax.dot_general(k2_ref[...], q[:, P_HALF:], NT_DIMS, preferred_element_type=F32, precision=hi)
    v1, i1, _ = _topk_rows(s1, P_TOPK)
    v2, i2, _ = _topk_rows(s2, P_TOPK)
    v2a = jnp.concatenate(v2, axis=0)
    i2a = jnp.concatenate(i2, axis=0)
    cand = jnp.concatenate([v1[a] + v2a for a in range(P_TOPK)], axis=0)
    cidx = jnp.concatenate([i1[a] * N_KEYS + i2a for a in range(P_TOPK)], axis=0)
    sc, _, eidx = _topk_rows(cand, P_TOPK, extra=cidx)
    sc = jnp.concatenate(sc, axis=0)
    p = jnp.exp(sc - jnp.max(sc, axis=0, keepdims=True))
    g_ref[...] = p / jnp.sum(p, axis=0, keepdims=True)
    e_ref[...] = jnp.concatenate(eidx, axis=0)


def _peer_sel(n2, wq_bf, qn_w, k1, k2):
    n, d = n2.shape
    tm = min(256, n)
    return pl.pallas_call(
        _peer_sel_kernel,
        grid=(n // tm, P_HEADS),
        in_specs=[pl.BlockSpec((tm, d), lambda i, h: (i, 0)),
                  pl.BlockSpec((d, P_QDIM), lambda i, h: (0, h)),
                  pl.BlockSpec((1, P_QDIM), lambda i, h: (0, 0)),
                  pl.BlockSpec((None, N_KEYS, P_HALF), lambda i, h: (h, 0, 0)),
                  pl.BlockSpec((None, N_KEYS, P_HALF), lambda i, h: (h, 0, 0))],
        out_specs=[pl.BlockSpec((None, P_TOPK, tm), lambda i, h: (h, 0, i)),
                   pl.BlockSpec((None, P_TOPK, tm), lambda i, h: (h, 0, i))],
        out_shape=[jax.ShapeDtypeStruct((P_HEADS, P_TOPK, n), jnp.int32),
                   jax.ShapeDtypeStruct((P_HEADS, P_TOPK, n), F32)],
        compiler_params=_cparams("parallel", "arbitrary"),
        name="peer_sel",
    )(n2, wq_bf, qn_w.reshape(1, P_QDIM), k1, k2)


def _pack_table(tab):
    e, d = tab.shape
    bits = lax.bitcast_convert_type(tab.astype(BF16), jnp.uint16).astype(jnp.uint32)
    bits = bits.reshape(e // 2, 2, d // LANES, LANES)
    return bits[:, 0] | (bits[:, 1] << 16)


def _expert_row(tab_ref, e):
    tile = tab_ref[e >> 1]
    sh = ((1 - (e & 1)) * 16).astype(jnp.uint32)
    return lax.bitcast_convert_type((tile << sh) & jnp.uint32(0xFFFF0000), F32)


def _gelu(x):
    return 0.5 * x * (1.0 + jnp.tanh(0.7978845608028654 * (x + 0.044715 * x * x * x)))


PAIR_UNROLL = 16


def _peer_u_kernel(idx_hbm, x_ref, g_ref, tab_ref, w_ref, idx_smem, sem, stage_ref, act_ref, *, tb):
    i = pl.program_id(0)
    cp = pltpu.make_async_copy(idx_hbm.at[pl.ds(i * tb * P_PAIRS, tb * P_PAIRS)], idx_smem, sem)
    cp.start()
    cp.wait()
    ones = jnp.ones((SUBLANES, LANES), F32)

    def tok(t, carry):
        x_t = x_ref[t]

        def chunk(c, carry2):
            base = t * P_PAIRS + c * PAIR_UNROLL
            for p in range(PAIR_UNROLL):
                row = _expert_row(tab_ref, idx_smem[base + p])
                stage_ref[pl.ds(pl.multiple_of((c * PAIR_UNROLL + p) * SUBLANES, SUBLANES), SUBLANES), :] = row * x_t
            return carry2

        lax.fori_loop(0, P_PAIRS // PAIR_UNROLL, chunk, 0)
        s = stage_ref[pl.ds(0, P_PAIRS, stride=SUBLANES), :]
        for k in range(1, SUBLANES):
            s = s + stage_ref[pl.ds(k, P_PAIRS, stride=SUBLANES), :]
        a = lax.dot_general(ones, s, NT_DIMS, preferred_element_type=F32, precision=lax.Precision.HIGHEST)
        act_ref[pl.ds(t, 1), :] = a[0:1, :]
        return carry

    lax.fori_loop(0, tb, tok, 0)
    w_ref[...] = g_ref[...] * _gelu(act_ref[...])


def _peer_u(idx_flat, x3, g2d, tab):
    n = x3.shape[0]
    tb = min(256, n)
    return pl.pallas_call(
        functools.partial(_peer_u_kernel, tb=tb),
        grid=(n // tb,),
        in_specs=[pl.BlockSpec(memory_space=pl.ANY),
                  pl.BlockSpec((tb, SUBLANES, LANES), lambda i: (i, 0, 0)),
                  pl.BlockSpec((tb, P_PAIRS), lambda i: (i, 0)),
                  pl.BlockSpec(tab.shape, lambda i: (0, 0, 0), pipeline_mode=pl.Buffered(1))],
        out_specs=pl.BlockSpec((tb, P_PAIRS), lambda i: (i, 0)),
        out_shape=jax.ShapeDtypeStruct((n, P_PAIRS), F32),
        scratch_shapes=[pltpu.SMEM((tb * P_PAIRS,), jnp.int32), pltpu.SemaphoreType.DMA(()),
                        pltpu.VMEM((P_PAIRS * SUBLANES, LANES), F32), pltpu.VMEM((tb, P_PAIRS), F32)],
        compiler_params=_cparams("arbitrary"),
        name="peer_u",
    )(idx_flat, x3, g2d, tab)


def _peer_v_kernel(idx_hbm, w_hbm, x1_ref, g2_ref, tab_ref, y_ref, idx_smem, w_smem, sem, out_ref, *, tb):
    i = pl.program_id(0)
    sl = pl.ds(i * tb * P_PAIRS, tb * P_PAIRS)
    cp_i = pltpu.make_async_copy(idx_hbm.at[sl], idx_smem, sem.at[0])
    cp_w = pltpu.make_async_copy(w_hbm.at[sl], w_smem, sem.at[1])
    cp_i.start()
    cp_w.start()
    cp_i.wait()
    cp_w.wait()
    nacc = 4

    def tok(t, carry):
        def chunk(c, accs):
            accs = list(accs)
            base = t * P_PAIRS + c * PAIR_UNROLL
            for p in range(PAIR_UNROLL):
                row = _expert_row(tab_ref, idx_smem[base + p])
                accs[p % nacc] = accs[p % nacc] + w_smem[base + p] * row
            return tuple(accs)

        zero = jnp.zeros((SUBLANES, LANES), F32)
        accs = lax.fori_loop(0, P_PAIRS // PAIR_UNROLL, chunk, (zero,) * nacc)
        out_ref[t] = (accs[0] + accs[1]) + (accs[2] + accs[3])
        return carry

    lax.fori_loop(0, tb, tok, 0)
    y_ref[...] = x1_ref[...] + g2_ref[...] * out_ref[...]


def _peer_v(idx_flat, w_flat, x13, g2, tab, rows_per_group):
    n = x13.shape[0]
    tb = min(256, n, rows_per_group if g2.ndim == 4 else n)
    if g2.ndim == 4:
        g2_spec = pl.BlockSpec((None, 1, SUBLANES, LANES), lambda i: (i * tb // rows_per_group, 0, 0, 0))
    else:
        g2_spec = pl.BlockSpec((tb, SUBLANES, LANES), lambda i: (i, 0, 0))
    return pl.pallas_call(
        functools.partial(_peer_v_kernel, tb=tb),
        grid=(n // tb,),
        in_specs=[pl.BlockSpec(memory_space=pl.ANY),
                  pl.BlockSpec(memory_space=pl.ANY),
                  pl.BlockSpec((tb, SUBLANES, LANES), lambda i: (i, 0, 0)),
                  g2_spec,
                  pl.BlockSpec(tab.shape, lambda i: (0, 0, 0), pipeline_mode=pl.Buffered(1))],
        out_specs=pl.BlockSpec((tb, SUBLANES, LANES), lambda i: (i, 0, 0)),
        out_shape=jax.ShapeDtypeStruct((n, SUBLANES, LANES), F32),
        scratch_shapes=[pltpu.SMEM((tb * P_PAIRS,), jnp.int32), pltpu.SMEM((tb * P_PAIRS,), F32),
                        pltpu.SemaphoreType.DMA((2,)), pltpu.VMEM((tb, SUBLANES, LANES), F32)],
        compiler_params=_cparams("arbitrary"),
        name="peer_v",
    )(idx_flat, w_flat, x13, g2, tab)


def _group_mod(m):
    return m[:, None, :]


def _layer(x, mod, pos, lb, s0, kv_bufs, wts):
    (norm1_w, norm2_w, w_in_bf, hgrn_norm_w, qk_w, wa_bf, wb_bf, wo_bf, wq_bf, peer_qn_w,
     peer_k1, peer_k2, u_tab, v_tab) = wts
    b, t, d = x.shape
    n = b * t
    sample = kv_bufs is not None
    sh1, sc1, g1, sh2, sc2, g2 = jnp.split(mod, 6, axis=-1)
    if sample:
        per_row = lambda m: jnp.repeat(m, t, axis=0)
        sh1, sc1, g1, sh2, sc2 = map(per_row, (sh1, sc1, g1, sh2, sc2))
        g2 = per_row(g2).reshape(n, SUBLANES, LANES)
    else:
        sh1, sc1, g1, sh2, sc2 = map(_group_mod, (sh1, sc1, g1, sh2, sc2))
        g2 = g2.reshape(b, 1, SUBLANES, LANES)
    x2 = x.reshape(n, d)
    h2 = _in_proj(x2, sc1, sh1, norm1_w, w_in_bf, t)
    h3 = h2.reshape(b, t, IN_COLS)

    if sample:
        tp = HGRN_SUB
        h3p = jnp.pad(h3, ((0, 0), (0, tp - t), (0, 0)))
        pos_p = pos[0] + jnp.arange(tp)
        oa, s_new = _hgrn(h3p, lb, s0, hgrn_norm_w, t_valid=t)
        cos_t, sin_t = _rope_tables(pos_p)
        qk3 = _qk_rope(h3p, qk_w, cos_t, sin_t)
        ob = _attn_sample(qk3, h3p, kv_bufs, t)
        oa, ob, qk3 = oa[:, :t], ob[:, :t], qk3[:, :t]
    else:
        oa, s_new = _hgrn(h3, lb, s0, hgrn_norm_w)
        cos_t, sin_t = _rope_tables(pos)
        qk3 = _qk_rope(h3, qk_w, cos_t, sin_t)
        ob = _attn_prompt(qk3, h3)

    kn = qk3[..., ATT_WIDTH:].reshape(b, t, N_GROUPS * G_HEADS, HEAD_DIM)
    voff = 4 * A_WIDTH + 2 * ATT_WIDTH
    vn = h3[..., voff:voff + ATT_WIDTH].reshape(b, t, N_GROUPS * G_HEADS, HEAD_DIM)
    new_kv = []
    for gi, (win, _) in enumerate(ATT_GROUPS):
        keep = t if sample else min(win, t)
        hs = slice(gi * G_HEADS, (gi + 1) * G_HEADS)
        kg = kn[:, t - keep:, hs].transpose(0, 2, 1, 3)
        vg = vn[:, t - keep:, hs].transpose(0, 2, 1, 3)
        new_kv.append(jnp.stack([kg, vg], axis=1))

    x1, n2 = _out_proj(oa.reshape(n, A_WIDTH), ob.reshape(n, G_HEADS * HEAD_DIM), h2, x2, g1, sc2, sh2,
                       norm2_w, wa_bf, wb_bf, wo_bf, t)
    e_t, g_t = _peer_sel(n2, wq_bf, peer_qn_w, peer_k1, peer_k2)
    idx_flat = e_t.reshape(P_PAIRS, n).T.reshape(n * P_PAIRS)
    g2d = g_t.reshape(P_PAIRS, n).T
    w = _peer_u(idx_flat, n2.reshape(n, SUBLANES, LANES), g2d, u_tab)
    y3 = _peer_v(idx_flat, w.reshape(n * P_PAIRS), x1.reshape(n, SUBLANES, LANES), g2, v_tab, t)
    return y3.reshape(b, t, d), new_kv, s_new


def kernel(x_prompt, x_sample, cache_kv_g1, cache_kv_g2, cache_kv_g3, state_hgrn, c_prompt, c_sample, w_ada, b_ada, norm1_w, norm2_w, w_in, lb_logits, hgrn_norm_w, q_norm_w, k_norm_w, w_branch_a, w_branch_b, w_out, peer_wq, peer_qn_w, peer_k1, peer_k2, peer_u, peer_v):
    depth = w_ada.shape[0]
    bp, tp_len, _ = x_prompt.shape
    bs, ts_len, _ = x_sample.shape
    pos_p = jnp.arange(tp_len)
    pos_s = PAST_LEN + jnp.arange(ts_len)
    lb_all = jnp.cumsum(jax.nn.softmax(lb_logits.astype(F32), axis=0), axis=0)
    caches = (cache_kv_g1, cache_kv_g2, cache_kv_g3)
    c_all = jnp.concatenate([c_prompt, c_sample], axis=0)
    yp, ys = x_prompt, x_sample
    kvp, kvs = ([], [], []), ([], [], [])
    sp_list, ss_list = [], []
    for l in range(depth):
        mod = _ada(c_all, w_ada[l], b_ada[l])
        wts = (norm1_w[l], norm2_w[l], w_in[l].astype(BF16), hgrn_norm_w[l],
               jnp.stack([q_norm_w[l], k_norm_w[l]])[:, None, :],
               w_branch_a[l].astype(BF16), w_branch_b[l].astype(BF16), w_out[l].astype(BF16),
               peer_wq[l].astype(BF16), peer_qn_w[l], peer_k1[l], peer_k2[l],
               _pack_table(peer_u[l]), _pack_table(peer_v[l]))
        lb = lb_all[l].reshape(A_HEADS, A_DK)
        s0p = jnp.zeros((bp, A_HEADS, A_DK, A_DK), F32)
        yp, nkv_p, sp = _layer(yp, mod[:bp], pos_p, lb, s0p, None, wts)
        ys, nkv_s, ss = _layer(ys, mod[bp:], pos_s, lb, state_hgrn[l],
                               tuple(c[l] for c in caches), wts)
        for gi in range(N_GROUPS):
            kvp[gi].append(nkv_p[gi])
            kvs[gi].append(nkv_s[gi])
        sp_list.append(sp)
        ss_list.append(ss)
    return (yp, ys, jnp.stack(kvp[0]), jnp.stack(kvp[1]), jnp.stack(kvp[2]), jnp.stack(sp_list),
            jnp.stack(kvs[0]), jnp.stack(kvs[1]), jnp.stack(kvs[2]), jnp.stack(ss_list))
```

```python
import functools

import jax
import jax.numpy as jnp
from jax import lax
from jax.experimental import pallas as pl
from jax.experimental.pallas import tpu as pltpu

F32 = jnp.float32
BF16 = jnp.bfloat16

D_MODEL = 1024
PAST_LEN = 16384
A_HEADS = 4
A_DK = 128
A_WIDTH = A_HEADS * A_DK
ATT_GROUPS = ((128, 1), (512, 4), (2048, 16))
N_GROUPS = 3
G_HEADS = 4
HEAD_DIM = 128
ATT_WIDTH = N_GROUPS * G_HEADS * HEAD_DIM
ATT_BLOCK = 128
ROT_DIM = HEAD_DIM // 4
ROPE_THETA = 500000.0
IN_COLS = 4 * A_WIDTH + 3 * ATT_WIDTH + 2 * D_MODEL
P_HEADS = 8
N_KEYS = 128
P_QDIM = 256
P_HALF = P_QDIM // 2
P_TOPK = 16
P_PAIRS = P_HEADS * P_TOPK
EPS = 1e-6

LANES = 128
SUBLANES = 8
HGRN_SUB = 16
ATT_TILE = ATT_BLOCK * ATT_GROUPS[-1][1]
VMEM_LIMIT = 56 * 1024 * 1024

NT_DIMS = (((1,), (1,)), ((), ()))
TN_DIMS = (((0,), (0,)), ((), ()))


def _cparams(*sem):
    return pltpu.CompilerParams(dimension_semantics=sem, vmem_limit_bytes=VMEM_LIMIT)


def _rms(x):
    return x * lax.rsqrt(jnp.mean(x * x, axis=-1, keepdims=True) + EPS)


def _sigmoid(x):
    return 1.0 / (1.0 + jnp.exp(-x))


def _mod_spec(arr, rows_per_group, tm, ngrid):
    if arr.ndim == 3:
        if ngrid == 2:
            return pl.BlockSpec((None, 1, arr.shape[-1]), lambda i, j: (i * tm // rows_per_group, 0, 0))
        return pl.BlockSpec((None, 1, arr.shape[-1]), lambda i: (i * tm // rows_per_group, 0, 0))
    if ngrid == 2:
        return pl.BlockSpec((tm, arr.shape[-1]), lambda i, j: (i, 0))
    return pl.BlockSpec((tm, arr.shape[-1]), lambda i: (i, 0))


def _ada_kernel(c_ref, w_ref, b_ref, o_ref):
    c = c_ref[...]
    s = c * _sigmoid(c)
    o_ref[...] = jnp.dot(s, w_ref[...], preferred_element_type=F32) + b_ref[...]


def _ada(c, w, b):
    rows, d = c.shape
    n = -(-rows // SUBLANES) * SUBLANES
    c = jnp.pad(c, ((0, n - rows), (0, 0)))
    cols = w.shape[1]
    tn = 1024
    out = pl.pallas_call(
        _ada_kernel,
        grid=(cols // tn,),
        in_specs=[pl.BlockSpec((n, d), lambda j: (0, 0)),
                  pl.BlockSpec((d, tn), lambda j: (0, j)),
                  pl.BlockSpec((1, tn), lambda j: (0, j))],
        out_specs=pl.BlockSpec((n, tn), lambda j: (0, j)),
        out_shape=jax.ShapeDtypeStruct((n, cols), F32),
        compiler_params=_cparams("arbitrary"),
        name="ada",
    )(c, w, b.reshape(1, cols))
    return out[:rows]


def _in_proj_kernel(x_ref, sc_ref, sh_ref, nw_ref, w_ref, o_ref, xn_ref):
    @pl.when(pl.program_id(1) == 0)
    def _():
        y = _rms(x_ref[...]) * nw_ref[...]
        y = y * (1.0 + sc_ref[...]) + sh_ref[...]
        xn_ref[...] = y.astype(BF16)

    o_ref[...] = jnp.dot(xn_ref[...], w_ref[...], preferred_element_type=F32)


def _in_proj(x2, sc, sh, nw, w_bf, rows_per_group):
    n, d = x2.shape
    cols = w_bf.shape[1]
    tm = min(1024, n, rows_per_group if sc.ndim == 3 else n)
    tn = 512
    return pl.pallas_call(
        _in_proj_kernel,
        grid=(n // tm, cols // tn),
        in_specs=[pl.BlockSpec((tm, d), lambda i, j: (i, 0)),
                  _mod_spec(sc, rows_per_group, tm, 2),
                  _mod_spec(sh, rows_per_group, tm, 2),
                  pl.BlockSpec((1, d), lambda i, j: (0, 0)),
                  pl.BlockSpec((d, tn), lambda i, j: (0, j))],
        out_specs=pl.BlockSpec((tm, tn), lambda i, j: (i, j)),
        out_shape=jax.ShapeDtypeStruct((n, cols), F32),
        scratch_shapes=[pltpu.VMEM((tm, d), BF16)],
        compiler_params=_cparams("parallel", "arbitrary"),
        name="in_proj",
    )(x2, sc, sh, nw.reshape(1, d), w_bf)


def _hgrn_kernel(q_ref, f_ref, i_ref, g_ref, lb_ref, s0_ref, nw_ref, o_ref, sout_ref,
                 st_ref, b_ref, k_ref, *, tt, sub, t_valid):
    t = pl.program_id(2)

    @pl.when(t == 0)
    def _():
        st_ref[...] = s0_ref[...].T

    lb = lb_ref[...]
    f = lb + (1.0 - lb) * _sigmoid(f_ref[...])
    logf = jnp.log(f)
    kk = 1.0 - f
    if t_valid is not None:
        live = lax.broadcasted_iota(jnp.int32, (tt, 1), 0) < t_valid
        logf = jnp.where(live, logf, 0.0)
        kk = jnp.where(live, kk, 0.0)
    r = lax.broadcasted_iota(jnp.int32, (tt, tt), 0)
    c = lax.broadcasted_iota(jnp.int32, (tt, tt), 1)
    tri = ((r // sub == c // sub) & (c <= r)).astype(F32)
    b_ref[...] = jnp.dot(tri, logf, preferred_element_type=F32, precision=lax.Precision.HIGHEST)
    k_ref[...] = kk

    row = lax.broadcasted_iota(jnp.int32, (sub, 1), 0)
    nw = nw_ref[...]

    def body(i, carry):
        sl = pl.ds(pl.multiple_of(i * sub, sub), sub)
        bq = b_ref[sl, :]
        kq = k_ref[sl, :]
        qq = q_ref[sl, :]
        vv = i_ref[sl, :]
        st = st_ref[...]
        qe = qq * jnp.exp(bq)
        o = lax.dot_general(qe.astype(BF16), st.astype(BF16), NT_DIMS, preferred_element_type=F32)
        for s in range(sub):
            dd = jnp.where(row >= s, bq - bq[s:s + 1, :], -jnp.inf)
            m = qq * kq[s:s + 1, :] * jnp.exp(dd)
            o = o + jnp.sum(m, axis=-1, keepdims=True) * vv[s:s + 1, :]
        bl = bq[sub - 1:sub, :]
        kd = kq * jnp.exp(bl - bq)
        st_ref[...] = st * jnp.exp(bl) + lax.dot_general(vv.astype(BF16), kd.astype(BF16), TN_DIMS,
                                                         preferred_element_type=F32)
        gq = g_ref[sl, :]
        o_ref[sl, :] = _rms(o) * nw * (gq * _sigmoid(gq))
        return carry

    lax.fori_loop(0, tt // sub, body, 0)

    @pl.when(t == pl.num_programs(2) - 1)
    def _():
        sout_ref[...] = st_ref[...].T


def _hgrn(h3, lb, s0, nw, t_valid=None):
    b, t, _ = h3.shape
    tt = min(256, t)
    sub = min(HGRN_SUB, tt)
    blk = lambda off: pl.BlockSpec((None, tt, A_DK), lambda bi, hi, ti: (bi, ti, off + hi))
    return pl.pallas_call(
        functools.partial(_hgrn_kernel, tt=tt, sub=sub, t_valid=t_valid),
        grid=(b, A_HEADS, t // tt),
        in_specs=[blk(0), blk(A_HEADS), blk(2 * A_HEADS), blk(3 * A_HEADS),
                  pl.BlockSpec((None, 1, A_DK), lambda bi, hi, ti: (hi, 0, 0)),
                  pl.BlockSpec((None, None, A_DK, A_DK), lambda bi, hi, ti: (bi, hi, 0, 0)),
                  pl.BlockSpec((1, A_DK), lambda bi, hi, ti: (0, 0))],
        out_specs=[pl.BlockSpec((None, tt, A_DK), lambda bi, hi, ti: (bi, ti, hi)),
                   pl.BlockSpec((None, None, A_DK, A_DK), lambda bi, hi, ti: (bi, hi, 0, 0))],
        out_shape=[jax.ShapeDtypeStruct((b, t, A_WIDTH), F32),
                   jax.ShapeDtypeStruct((b, A_HEADS, A_DK, A_DK), F32)],
        scratch_shapes=[pltpu.VMEM((A_DK, A_DK), F32), pltpu.VMEM((tt, A_DK), F32),
                        pltpu.VMEM((tt, A_DK), F32)],
        compiler_params=_cparams("parallel", "parallel", "arbitrary"),
        name="hgrn",
    )(h3, h3, h3, h3, lb.reshape(A_HEADS, 1, A_DK), s0, nw.reshape(1, A_DK))


ROPE_HEADS = 4


def _qk_rope_kernel(h_ref, w_ref, cos_ref, sin_ref, o_ref):
    half = ROT_DIM // 2
    lane = lax.broadcasted_iota(jnp.int32, cos_ref.shape, 1)
    for k in range(ROPE_HEADS):
        cols = slice(k * HEAD_DIM, (k + 1) * HEAD_DIM)
        y = _rms(h_ref[:, cols]) * w_ref[...]
        partner = jnp.where(lane < half, pltpu.roll(y, LANES - half, axis=1), pltpu.roll(y, half, axis=1))
        o_ref[:, cols] = y * cos_ref[...] + partner * sin_ref[...]


def _rope_tables(pos):
    inv = ROPE_THETA ** (-jnp.arange(0, ROT_DIM, 2, dtype=F32) / ROT_DIM)
    ang = pos.astype(F32)[:, None] * inv[None, :]
    cos, sin = jnp.cos(ang), jnp.sin(ang)
    rest = HEAD_DIM - ROT_DIM
    cos_t = jnp.concatenate([cos, cos, jnp.ones((pos.shape[0], rest), F32)], axis=-1)
    sin_t = jnp.concatenate([-sin, sin, jnp.zeros((pos.shape[0], rest), F32)], axis=-1)
    return cos_t, sin_t


def _qk_rope(h3, qk_w, cos_t, sin_t):
    b, t, _ = h3.shape
    tt = min(1024, t)
    width = ROPE_HEADS * HEAD_DIM
    nq = ATT_WIDTH // width
    off = 4 * A_WIDTH // width
    return pl.pallas_call(
        _qk_rope_kernel,
        grid=(b, t // tt, 2 * nq),
        in_specs=[pl.BlockSpec((None, tt, width), lambda bi, ti, hi: (bi, ti, off + hi)),
                  pl.BlockSpec((None, 1, HEAD_DIM), lambda bi, ti, hi: (hi // nq, 0, 0)),
                  pl.BlockSpec((tt, HEAD_DIM), lambda bi, ti, hi: (ti, 0)),
                  pl.BlockSpec((tt, HEAD_DIM), lambda bi, ti, hi: (ti, 0))],
        out_specs=pl.BlockSpec((None, tt, width), lambda bi, ti, hi: (bi, ti, hi)),
        out_shape=jax.ShapeDtypeStruct((b, t, 2 * ATT_WIDTH), F32),
        compiler_params=_cparams("parallel", "parallel", "arbitrary"),
        name="qk_rope",
    )(h3, qk_w, cos_t, sin_t)


def _softmax_block(s_list, v_list):
    m = s_list[0].max(axis=-1, keepdims=True)
    for s in s_list[1:]:
        m = jnp.maximum(m, s.max(axis=-1, keepdims=True))
    den = None
    o = None
    for s, v in zip(s_list, v_list):
        p = jnp.exp(s - m)
        d = jnp.sum(p, axis=-1, keepdims=True)
        pv = jnp.dot(p.astype(BF16), v.astype(BF16), preferred_element_type=F32)
        den = d if den is None else den + d
        o = pv if o is None else o + pv
    return o / den, m + jnp.log(den)


def _attn_kernel(*refs, tq):
    ins = refs[:15]
    o_ref = refs[15]
    og_refs, lse_refs = refs[16:16 + N_GROUPS], refs[16 + N_GROUPS:16 + 2 * N_GROUPS]
    ti = pl.program_id(2)
    scale = HEAD_DIM ** -0.5
    qi = lax.broadcasted_iota(jnp.int32, (ATT_BLOCK, ATT_BLOCK), 0)
    ki = lax.broadcasted_iota(jnp.int32, (ATT_BLOCK, ATT_BLOCK), 1)
    for g, (win, dil) in enumerate(ATT_GROUPS):
        q_ref, kc_ref, kp_ref, vc_ref, vp_ref = ins[5 * g:5 * g + 5]
        span = ATT_BLOCK * dil
        nblk = tq // span

        def load(ref, start, dil=dil):
            if dil == 1:
                return ref[pl.ds(start, ATT_BLOCK), :]
            return ref[pl.ds(start, ATT_BLOCK, stride=dil), :]

        def body(i, carry, dil=dil, span=span, nblk=nblk, og_ref=og_refs[g], lse_ref=lse_refs[g],
                 q_ref=q_ref, kc_ref=kc_ref, kp_ref=kp_ref, vc_ref=vc_ref, vp_ref=vp_ref, load=load):
            r = i // nblk
            n = i % nblk
            start = r + span * n
            qb = load(q_ref, start).astype(BF16)
            kc = load(kc_ref, start)
            vc = load(vc_ref, start)
            last = r + span * (nblk - 1)
            if nblk == 1:
                kp = load(kp_ref, last)
                vp = load(vp_ref, last)
                has_prev = ti > 0
            else:
                inner = jnp.maximum(start - span, r)
                first = n == 0
                kp = jnp.where(first, load(kp_ref, last), load(kc_ref, inner))
                vp = jnp.where(first, load(vp_ref, last), load(vc_ref, inner))
                has_prev = jnp.logical_or(ti > 0, n > 0)
            s_c = lax.dot_general(qb, kc.astype(BF16), NT_DIMS, preferred_element_type=F32) * scale
            s_p = lax.dot_general(qb, kp.astype(BF16), NT_DIMS, preferred_element_type=F32) * scale
            s_c = jnp.where(ki <= qi, s_c, -jnp.inf)
            s_p = jnp.where(jnp.logical_and(ki >= qi, has_prev), s_p, -jnp.inf)
            o, lse = _softmax_block([s_c, s_p], [vc, vp])
            if dil == 1:
                og_ref[pl.ds(start, ATT_BLOCK), :] = o
                lse_ref[pl.ds(start, ATT_BLOCK), :] = jnp.broadcast_to(lse, o.shape)
            else:
                og_ref[pl.ds(start, ATT_BLOCK, stride=dil), :] = o
                lse_ref[pl.ds(start, ATT_BLOCK, stride=dil), :] = jnp.broadcast_to(lse, o.shape)
            return carry

        lax.fori_loop(0, tq // ATT_BLOCK, body, 0)

    l0, l1, l2 = lse_refs[0][...], lse_refs[1][...], lse_refs[2][...]
    m = jnp.maximum(jnp.maximum(l0, l1), l2)
    e0, e1, e2 = jnp.exp(l0 - m), jnp.exp(l1 - m), jnp.exp(l2 - m)
    z = e0 + e1 + e2
    o_ref[...] = (e0 / z) * og_refs[0][...] + (e1 / z) * og_refs[1][...] + (e2 / z) * og_refs[2][...]


def _attn_prompt(qk3, h3):
    b, t, _ = qk3.shape
    tq = ATT_TILE
    assert t % tq == 0
    nh = N_GROUPS * G_HEADS
    voff = (4 * A_WIDTH + 2 * ATT_WIDTH) // HEAD_DIM
    cur = lambda off: pl.BlockSpec((None, tq, HEAD_DIM), lambda bi, ji, ti, off=off: (bi, ti, off + ji))
    prev = lambda off: pl.BlockSpec((None, tq, HEAD_DIM),
                                    lambda bi, ji, ti, off=off: (bi, jnp.maximum(ti - 1, 0), off + ji))
    in_specs, args = [], []
    for g in range(N_GROUPS):
        in_specs += [cur(g * G_HEADS), cur(nh + g * G_HEADS), prev(nh + g * G_HEADS),
                     cur(voff + g * G_HEADS), prev(voff + g * G_HEADS)]
        args += [qk3, qk3, qk3, h3, h3]
    return pl.pallas_call(
        functools.partial(_attn_kernel, tq=tq),
        grid=(b, G_HEADS, t // tq),
        in_specs=in_specs,
        out_specs=pl.BlockSpec((None, tq, HEAD_DIM), lambda bi, ji, ti: (bi, ti, ji)),
        out_shape=jax.ShapeDtypeStruct((b, t, G_HEADS * HEAD_DIM), F32),
        scratch_shapes=[pltpu.VMEM((tq, HEAD_DIM), F32)] * (2 * N_GROUPS),
        compiler_params=_cparams("parallel", "parallel", "arbitrary"),
        name="attn_prompt",
    )(*args)


def _attn_sample_kernel(*refs, tp, t_valid):
    o_ref = refs[12]
    scale = HEAD_DIM ** -0.5
    outs, lses = [], []
    for g, (win, dil) in enumerate(ATT_GROUPS):
        c_ref, q_ref, k_ref, v_ref = refs[4 * g:4 * g + 4]
        n_back = win // dil
        w = c_ref.shape[1]
        qb = q_ref[...].astype(BF16)
        s_b = lax.dot_general(qb, c_ref[0].astype(BF16), NT_DIMS, preferred_element_type=F32) * scale
        s_n = lax.dot_general(qb, k_ref[...].astype(BF16), NT_DIMS, preferred_element_type=F32) * scale
        tq_b = lax.broadcasted_iota(jnp.int32, (tp, w), 0)
        rb = lax.broadcasted_iota(jnp.int32, (tp, w), 1)
        delta = w + tq_b - rb
        ok_b = jnp.logical_and(delta % dil == 0, delta // dil <= n_back)
        tq_n = lax.broadcasted_iota(jnp.int32, (tp, tp), 0)
        tn = lax.broadcasted_iota(jnp.int32, (tp, tp), 1)
        dn = tq_n - tn
        ok_n = (dn >= 0) & (dn % dil == 0) & (dn // dil <= n_back) & (tn < t_valid)
        s_b = jnp.where(ok_b, s_b, -jnp.inf)
        s_n = jnp.where(ok_n, s_n, -jnp.inf)
        o, lse = _softmax_block([s_b, s_n], [c_ref[1], v_ref[...]])
        outs.append(o)
        lses.append(lse)
    m = jnp.maximum(jnp.maximum(lses[0], lses[1]), lses[2])
    es = [jnp.exp(l - m) for l in lses]
    z = es[0] + es[1] + es[2]
    o_ref[...] = (es[0] / z) * outs[0] + (es[1] / z) * outs[1] + (es[2] / z) * outs[2]


def _attn_sample(qk3, h3, caches, t_valid):
    b, tp, _ = qk3.shape
    nh = N_GROUPS * G_HEADS
    voff = (4 * A_WIDTH + 2 * ATT_WIDTH) // HEAD_DIM
    row = lambda off: pl.BlockSpec((None, tp, HEAD_DIM), lambda bi, ji, off=off: (bi, 0, off + ji))
    in_specs, args = [], []
    for g in range(N_GROUPS):
        w = caches[g].shape[3]
        in_specs += [pl.BlockSpec((None, 2, None, w, HEAD_DIM), lambda bi, ji: (bi, 0, ji, 0, 0)),
                     row(g * G_HEADS), row(nh + g * G_HEADS), row(voff + g * G_HEADS)]
        args += [caches[g], qk3, qk3, h3]
    return pl.pallas_call(
        functools.partial(_attn_sample_kernel, tp=tp, t_valid=t_valid),
        grid=(b, G_HEADS),
        in_specs=in_specs,
        out_specs=pl.BlockSpec((None, tp, HEAD_DIM), lambda bi, ji: (bi, 0, ji)),
        out_shape=jax.ShapeDtypeStruct((b, tp, G_HEADS * HEAD_DIM), F32),
        compiler_params=_cparams("parallel", "arbitrary"),
        name="attn_sample",
    )(*args)


def _out_proj_kernel(oa_ref, ob_ref, ga0_ref, ga1_ref, gb0_ref, gb1_ref, x_ref, g1_ref, sc_ref, sh_ref,
                     nw_ref, wa_ref, wb_ref, wo_ref, x1_ref, n2_ref):
    ya = jnp.dot(oa_ref[...].astype(BF16), wa_ref[...], preferred_element_type=F32)
    yb = jnp.dot(ob_ref[...].astype(BF16), wb_ref[...], preferred_element_type=F32)
    gate_a = _sigmoid(jnp.concatenate([ga0_ref[...], ga1_ref[...]], axis=-1))
    gate_b = _sigmoid(jnp.concatenate([gb0_ref[...], gb1_ref[...]], axis=-1))
    mix = jnp.dot((gate_a * ya + gate_b * yb).astype(BF16), wo_ref[...], preferred_element_type=F32)
    x1 = x_ref[...] + g1_ref[...] * mix
    x1_ref[...] = x1
    n2_ref[...] = (_rms(x1) * nw_ref[...]) * (1.0 + sc_ref[...]) + sh_ref[...]


def _out_proj(oa2, ob2, h2, x2, g1, sc2, sh2, nw2, wa_bf, wb_bf, wo_bf, rows_per_group):
    n, d = x2.shape
    tm = min(512, n, rows_per_group if g1.ndim == 3 else n)
    gw = 512
    goff = (4 * A_WIDTH + 3 * ATT_WIDTH) // gw
    rows = lambda width: pl.BlockSpec((tm, width), lambda i: (i, 0))
    gcol = lambda k: pl.BlockSpec((tm, gw), lambda i, k=k: (i, goff + k))
    full = lambda a: pl.BlockSpec(a.shape, lambda i: (0, 0))
    mod = lambda a: _mod_spec(a, rows_per_group, tm, 1)
    return pl.pallas_call(
        _out_proj_kernel,
        grid=(n // tm,),
        in_specs=[rows(A_WIDTH), rows(G_HEADS * HEAD_DIM), gcol(0), gcol(1), gcol(2), gcol(3), rows(d),
                  mod(g1), mod(sc2), mod(sh2), pl.BlockSpec((1, d), lambda i: (0, 0)),
                  full(wa_bf), full(wb_bf), full(wo_bf)],
        out_specs=[rows(d), rows(d)],
        out_shape=[jax.ShapeDtypeStruct((n, d), F32), jax.ShapeDtypeStruct((n, d), F32)],
        compiler_params=_cparams("parallel"),
        name="out_proj",
    )(oa2, ob2, h2, h2, h2, h2, x2, g1, sc2, sh2, nw2.reshape(1, d), wa_bf, wb_bf, wo_bf)


def _topk_rows(s, k, extra=None):
    rows = s.shape[0]
    ridx = lax.broadcasted_iota(jnp.int32, s.shape, 0).astype(F32)
    vals, idxs, extras = [], [], []
    for _ in range(k):
        m = jnp.max(s, axis=0, keepdims=True)
        idx = jnp.min(jnp.where(s == m, ridx, float(rows)), axis=0, keepdims=True)
        hit = ridx == idx
        vals.append(m)
        idxs.append(idx.astype(jnp.int32))
        if extra is not None:
            extras.append(jnp.sum(jnp.where(hit, extra, 0), axis=0, keepdims=True))
        s = jnp.where(hit, -jnp.inf, s)
    return vals, idxs, extras


CAND_CELLS = tuple((a, b) for a in range(P_TOPK) for b in range(P_TOPK) if (a + 1) * (b + 1) <= P_TOPK)
CAND_PAD = -len(CAND_CELLS) % SUBLANES
SEL_TILE = 128


def _peer_sel_kernel(n2_ref, wq_ref, qn_ref, k1_ref, k2_ref, e_ref, g_ref, q_ref, *, tm):
    q = jnp.dot(n2_ref[...].astype(BF16), wq_ref[...], preferred_element_type=F32)
    q_ref[...] = _rms(q) * qn_ref[...]
    hi = lax.Precision.HIGHEST

    def tile(ti, carry):
        q = q_ref[pl.ds(pl.multiple_of(ti * SEL_TILE, SEL_TILE), SEL_TILE), :]
        s1 = lax.dot_general(k1_ref[...], q[:, :P_HALF], NT_DIMS, preferred_element_type=F32, precision=hi)
        s2 = lax.dot_general(k2_ref[...], q[:, P_HALF:], NT_DIMS, preferred_element_type=F32, precision=hi)
        v1, i1, _ = _topk_rows(s1, P_TOPK)
        v2, i2, _ = _topk_rows(s2, P_TOPK)
        cand = [v1[a] + v2[b] for a, b in CAND_CELLS]
        cidx = [i1[a] * N_KEYS + i2[b] for a, b in CAND_CELLS]
        cand += [jnp.full_like(cand[0], -jnp.inf)] * CAND_PAD
        cidx += [jnp.zeros_like(cidx[0])] * CAND_PAD
        sc, _, eidx = _topk_rows(jnp.concatenate(cand, axis=0), P_TOPK, extra=jnp.concatenate(cidx, axis=0))
        sc = jnp.concatenate(sc, axis=0)
        p = jnp.exp(sc - jnp.max(sc, axis=0, keepdims=True))
        cols = pl.ds(pl.multiple_of(ti * SEL_TILE, SEL_TILE), SEL_TILE)
        g_ref[:, cols] = p / jnp.sum(p, axis=0, keepdims=True)
        e_ref[:, cols] = jnp.concatenate(eidx, axis=0)
        return carry

    lax.fori_loop(0, tm // SEL_TILE, tile, 0)


def _peer_sel(n2, wq_bf, qn_w, k1, k2):
    n, d = n2.shape
    tm = min(256, n)
    assert tm % SEL_TILE == 0
    return pl.pallas_call(
        functools.partial(_peer_sel_kernel, tm=tm),
        grid=(n // tm, P_HEADS),
        in_specs=[pl.BlockSpec((tm, d), lambda i, h: (i, 0)),
                  pl.BlockSpec((d, P_QDIM), lambda i, h: (0, h)),
                  pl.BlockSpec((1, P_QDIM), lambda i, h: (0, 0)),
                  pl.BlockSpec((None, N_KEYS, P_HALF), lambda i, h: (h, 0, 0)),
                  pl.BlockSpec((None, N_KEYS, P_HALF), lambda i, h: (h, 0, 0))],
        out_specs=[pl.BlockSpec((None, P_TOPK, tm), lambda i, h: (h, 0, i)),
                   pl.BlockSpec((None, P_TOPK, tm), lambda i, h: (h, 0, i))],
        out_shape=[jax.ShapeDtypeStruct((P_HEADS, P_TOPK, n), jnp.int32),
                   jax.ShapeDtypeStruct((P_HEADS, P_TOPK, n), F32)],
        scratch_shapes=[pltpu.VMEM((tm, P_QDIM), F32)],
        compiler_params=_cparams("parallel", "arbitrary"),
        name="peer_sel",
    )(n2, wq_bf, qn_w.reshape(1, P_QDIM), k1, k2)


def _pack_table(tab):
    e, d = tab.shape
    bits = lax.bitcast_convert_type(tab.astype(BF16), jnp.uint16).astype(jnp.uint32)
    bits = bits.reshape(e // 2, 2, d // LANES, LANES)
    return (bits[:, 0] | (bits[:, 1] << 16)).reshape(e // 2 * (d // LANES), LANES)


HI_HALF = 0xFFFF0000
EVEN_BIT = 4


def _index_word(e):
    return ((e >> 1) * SUBLANES) | ((1 - (e & 1)) * EVEN_BIT)


def _expert_row(tab_ref, word):
    tile = tab_ref[pl.ds(pl.multiple_of(word & ~(SUBLANES - 1), SUBLANES), SUBLANES), :]
    wv = jnp.full((SUBLANES, LANES), word, jnp.int32).astype(jnp.uint32)
    sh = (wv & jnp.uint32(EVEN_BIT)) << jnp.uint32(2)
    return lax.bitcast_convert_type((tile << sh) & jnp.uint32(HI_HALF), F32)


def _gelu(x):
    return 0.5 * x * (1.0 + jnp.tanh(0.7978845608028654 * (x + 0.044715 * x * x * x)))


def _smem_block_copies(srcs, dsts, sem, tb):
    rows = pl.ds(pl.program_id(0) * (tb * P_PAIRS), tb * P_PAIRS)
    return [pltpu.make_async_copy(src.at[rows], dst, sem.at[k]) for k, (src, dst) in enumerate(zip(srcs, dsts))]


PAIR_CHUNK = 32
N_CHUNKS = P_PAIRS // PAIR_CHUNK
U_CHUNK = 32
U_CHUNKS = P_PAIRS // U_CHUNK
U_FOLD = SUBLANES // U_CHUNKS
U_GROUP = 8


def _peer_u_row(tab_ref, idx_smem, x_ref, t, fill_ref, sum_ref):
    x_t = x_ref[t]

    def chunk(c, part):
        q0 = c * U_CHUNK
        base = t * P_PAIRS + q0
        for p in range(U_CHUNK):
            row = _expert_row(tab_ref, idx_smem[base + p])
            fill_ref[pl.ds(pl.multiple_of((q0 + p) * SUBLANES, SUBLANES), SUBLANES), :] = row * x_t
        for k in range(U_FOLD):
            part = part + sum_ref[pl.ds(c * U_FOLD + k, P_PAIRS, stride=SUBLANES), :]
        return part

    return lax.fori_loop(0, U_CHUNKS, chunk, jnp.zeros((P_PAIRS, LANES), F32))


def _peer_u_kernel(idx_hbm, x_ref, g_ref, tab_ref, w_ref, idx_smem, sem, stage_a, stage_b, part_ref, act_ref, *, tb):
    copies = _smem_block_copies((idx_hbm,), (idx_smem,), sem, tb)
    for cp in copies:
        cp.start()
    for cp in copies:
        cp.wait()
    stage_b[...] = jnp.zeros(stage_b.shape, F32)

    def two_rows(j, carry):
        t0 = 2 * j
        part_ref[jnp.maximum(t0 - 1, 0)] = _peer_u_row(tab_ref, idx_smem, x_ref, t0, stage_a, stage_b)
        part_ref[t0] = _peer_u_row(tab_ref, idx_smem, x_ref, t0 + 1, stage_b, stage_a)
        return carry

    lax.fori_loop(0, tb // 2, two_rows, 0)
    last = stage_b[pl.ds(0, P_PAIRS, stride=SUBLANES), :]
    for k in range(1, SUBLANES):
        last = last + stage_b[pl.ds(k, P_PAIRS, stride=SUBLANES), :]
    part_ref[tb - 1] = last

    eye = (lax.broadcasted_iota(jnp.int32, (P_PAIRS, LANES), 0)
           == lax.broadcasted_iota(jnp.int32, (P_PAIRS, LANES), 1))

    def lane_sums(gi, carry):
        rows = []
        for k in range(U_GROUP):
            col = jnp.sum(part_ref[gi * U_GROUP + k], axis=1, keepdims=True)
            rows.append(jnp.sum(jnp.where(eye, col, 0.0), axis=0, keepdims=True))
        act_ref[pl.ds(pl.multiple_of(gi * U_GROUP, U_GROUP), U_GROUP), :] = jnp.concatenate(rows, axis=0)
        return carry

    lax.fori_loop(0, tb // U_GROUP, lane_sums, 0)
    w_ref[...] = g_ref[...] * _gelu(act_ref[...])


def _peer_u(idx, x3, g2d, tab):
    n = x3.shape[0]
    tb = min(128, n)
    assert tb % U_GROUP == 0 and n % tb == 0
    return pl.pallas_call(
        functools.partial(_peer_u_kernel, tb=tb),
        grid=(n // tb,),
        in_specs=[pl.BlockSpec(memory_space=pl.ANY),
                  pl.BlockSpec((tb, SUBLANES, LANES), lambda i: (i, 0, 0)),
                  pl.BlockSpec((tb, P_PAIRS), lambda i: (i, 0)),
                  pl.BlockSpec(tab.shape, lambda i: (0, 0), pipeline_mode=pl.Buffered(1))],
        out_specs=pl.BlockSpec((tb, P_PAIRS), lambda i: (i, 0)),
        out_shape=jax.ShapeDtypeStruct((n, P_PAIRS), F32),
        scratch_shapes=[pltpu.SMEM((tb * P_PAIRS,), jnp.int32), pltpu.SemaphoreType.DMA((1,)),
                        pltpu.VMEM((P_PAIRS * SUBLANES, LANES), F32),
                        pltpu.VMEM((P_PAIRS * SUBLANES, LANES), F32),
                        pltpu.VMEM((tb, P_PAIRS, LANES), F32),
                        pltpu.VMEM((tb, P_PAIRS), F32)],
        compiler_params=_cparams("arbitrary"),
        name="peer_u",
    )(idx, x3, g2d, tab)


def _peer_v_kernel(idx_hbm, w_hbm, x1_ref, g2_ref, tab_ref, y_ref, idx_smem, w_smem, sem, out_ref, *, tb):
    copies = _smem_block_copies((idx_hbm, w_hbm), (idx_smem, w_smem), sem, tb)
    for cp in copies:
        cp.start()
    for cp in copies:
        cp.wait()
    nacc = 4

    def tok(t, carry):
        def chunk(c, accs):
            accs = list(accs)
            base = t * P_PAIRS + c * PAIR_CHUNK
            for p in range(PAIR_CHUNK):
                accs[p % nacc] = accs[p % nacc] + w_smem[base + p] * _expert_row(tab_ref, idx_smem[base + p])
            return tuple(accs)

        zero = jnp.zeros((SUBLANES, LANES), F32)
        accs = lax.fori_loop(0, N_CHUNKS, chunk, (zero,) * nacc)
        out_ref[t] = (accs[0] + accs[1]) + (accs[2] + accs[3])
        return carry

    lax.fori_loop(0, tb, tok, 0)
    y_ref[...] = x1_ref[...] + g2_ref[...] * out_ref[...]


def _peer_v(idx, w2d, x13, g2, tab, rows_per_group):
    n = x13.shape[0]
    tb = min(256, n, rows_per_group if g2.ndim == 4 else n)
    if g2.ndim == 4:
        g2_spec = pl.BlockSpec((None, 1, SUBLANES, LANES), lambda i: (i * tb // rows_per_group, 0, 0, 0))
    else:
        g2_spec = pl.BlockSpec((tb, SUBLANES, LANES), lambda i: (i, 0, 0))
    return pl.pallas_call(
        functools.partial(_peer_v_kernel, tb=tb),
        grid=(n // tb,),
        in_specs=[pl.BlockSpec(memory_space=pl.ANY),
                  pl.BlockSpec(memory_space=pl.ANY),
                  pl.BlockSpec((tb, SUBLANES, LANES), lambda i: (i, 0, 0)),
                  g2_spec,
                  pl.BlockSpec(tab.shape, lambda i: (0, 0), pipeline_mode=pl.Buffered(1))],
        out_specs=pl.BlockSpec((tb, SUBLANES, LANES), lambda i: (i, 0, 0)),
        out_shape=jax.ShapeDtypeStruct((n, SUBLANES, LANES), F32),
        scratch_shapes=[pltpu.SMEM((tb * P_PAIRS,), jnp.int32), pltpu.SMEM((tb * P_PAIRS,), F32),
                        pltpu.SemaphoreType.DMA((2,)), pltpu.VMEM((tb, SUBLANES, LANES), F32)],
        compiler_params=_cparams("arbitrary"),
        name="peer_v",
    )(idx, w2d, x13, g2, tab)


def _group_mod(m):
    return m[:, None, :]


def _layer(x, mod, pos, lb, s0, kv_bufs, wts):
    (norm1_w, norm2_w, w_in_bf, hgrn_norm_w, qk_w, wa_bf, wb_bf, wo_bf, wq_bf, peer_qn_w,
     peer_k1, peer_k2, u_tab, v_tab) = wts
    b, t, d = x.shape
    n = b * t
    sample = kv_bufs is not None
    sh1, sc1, g1, sh2, sc2, g2 = jnp.split(mod, 6, axis=-1)
    if sample:
        per_row = lambda m: jnp.repeat(m, t, axis=0)
        sh1, sc1, g1, sh2, sc2 = map(per_row, (sh1, sc1, g1, sh2, sc2))
        g2 = per_row(g2).reshape(n, SUBLANES, LANES)
    else:
        sh1, sc1, g1, sh2, sc2 = map(_group_mod, (sh1, sc1, g1, sh2, sc2))
        g2 = g2.reshape(b, 1, SUBLANES, LANES)
    x2 = x.reshape(n, d)
    h2 = _in_proj(x2, sc1, sh1, norm1_w, w_in_bf, t)
    h3 = h2.reshape(b, t, IN_COLS)

    if sample:
        tp = HGRN_SUB
        h3p = jnp.pad(h3, ((0, 0), (0, tp - t), (0, 0)))
        pos_p = pos[0] + jnp.arange(tp)
        oa, s_new = _hgrn(h3p, lb, s0, hgrn_norm_w, t_valid=t)
        cos_t, sin_t = _rope_tables(pos_p)
        qk3 = _qk_rope(h3p, qk_w, cos_t, sin_t)
        ob = _attn_sample(qk3, h3p, kv_bufs, t)
        oa, ob, qk3 = oa[:, :t], ob[:, :t], qk3[:, :t]
    else:
        oa, s_new = _hgrn(h3, lb, s0, hgrn_norm_w)
        cos_t, sin_t = _rope_tables(pos)
        qk3 = _qk_rope(h3, qk_w, cos_t, sin_t)
        ob = _attn_prompt(qk3, h3)

    kn = qk3[..., ATT_WIDTH:].reshape(b, t, N_GROUPS * G_HEADS, HEAD_DIM)
    voff = 4 * A_WIDTH + 2 * ATT_WIDTH
    vn = h3[..., voff:voff + ATT_WIDTH].reshape(b, t, N_GROUPS * G_HEADS, HEAD_DIM)
    new_kv = []
    for gi, (win, _) in enumerate(ATT_GROUPS):
        keep = t if sample else min(win, t)
        hs = slice(gi * G_HEADS, (gi + 1) * G_HEADS)
        kg = kn[:, t - keep:, hs].transpose(0, 2, 1, 3)
        vg = vn[:, t - keep:, hs].transpose(0, 2, 1, 3)
        new_kv.append(jnp.stack([kg, vg], axis=1))

    x1, n2 = _out_proj(oa.reshape(n, A_WIDTH), ob.reshape(n, G_HEADS * HEAD_DIM), h2, x2, g1, sc2, sh2,
                       norm2_w, wa_bf, wb_bf, wo_bf, t)
    e_t, g_t = _peer_sel(n2, wq_bf, peer_qn_w, peer_k1, peer_k2)
    idx = _index_word(e_t.reshape(P_PAIRS, n).T).reshape(n * P_PAIRS)
    g2d = g_t.reshape(P_PAIRS, n).T
    w = _peer_u(idx, n2.reshape(n, SUBLANES, LANES), g2d, u_tab)
    y3 = _peer_v(idx, w.reshape(n * P_PAIRS), x1.reshape(n, SUBLANES, LANES), g2, v_tab, t)
    return y3.reshape(b, t, d), new_kv, s_new


def kernel(x_prompt, x_sample, cache_kv_g1, cache_kv_g2, cache_kv_g3, state_hgrn, c_prompt, c_sample, w_ada, b_ada, norm1_w, norm2_w, w_in, lb_logits, hgrn_norm_w, q_norm_w, k_norm_w, w_branch_a, w_branch_b, w_out, peer_wq, peer_qn_w, peer_k1, peer_k2, peer_u, peer_v):
    depth = w_ada.shape[0]
    bp, tp_len, _ = x_prompt.shape
    bs, ts_len, _ = x_sample.shape
    pos_p = jnp.arange(tp_len)
    pos_s = PAST_LEN + jnp.arange(ts_len)
    lb_all = jnp.cumsum(jax.nn.softmax(lb_logits.astype(F32), axis=0), axis=0)
    caches = (cache_kv_g1, cache_kv_g2, cache_kv_g3)
    c_all = jnp.concatenate([c_prompt, c_sample], axis=0)
    yp, ys = x_prompt, x_sample
    kvp, kvs = ([], [], []), ([], [], [])
    sp_list, ss_list = [], []
    for l in range(depth):
        mod = _ada(c_all, w_ada[l], b_ada[l])
        wts = (norm1_w[l], norm2_w[l], w_in[l].astype(BF16), hgrn_norm_w[l],
               jnp.stack([q_norm_w[l], k_norm_w[l]])[:, None, :],
               w_branch_a[l].astype(BF16), w_branch_b[l].astype(BF16), w_out[l].astype(BF16),
               peer_wq[l].astype(BF16), peer_qn_w[l], peer_k1[l], peer_k2[l],
               _pack_table(peer_u[l]), _pack_table(peer_v[l]))
        lb = lb_all[l].reshape(A_HEADS, A_DK)
        s0p = jnp.zeros((bp, A_HEADS, A_DK, A_DK), F32)
        yp, nkv_p, sp = _layer(yp, mod[:bp], pos_p, lb, s0p, None, wts)
        ys, nkv_s, ss = _layer(ys, mod[bp:], pos_s, lb, state_hgrn[l],
                               tuple(c[l] for c in caches), wts)
        for gi in range(N_GROUPS):
            kvp[gi].append(nkv_p[gi])
            kvs[gi].append(nkv_s[gi])
        sp_list.append(sp)
        ss_list.append(ss)
    return (yp, ys, jnp.stack(kvp[0]), jnp.stack(kvp[1]), jnp.stack(kvp[2]), jnp.stack(sp_list),
            jnp.stack(kvs[0]), jnp.stack(kvs[1]), jnp.stack(kvs[2]), jnp.stack(ss_list))
```

```python
import functools

import jax
import jax.numpy as jnp
from jax import lax
from jax.experimental import pallas as pl
from jax.experimental.pallas import tpu as pltpu

F32 = jnp.float32
BF16 = jnp.bfloat16

D_MODEL = 1024
PAST_LEN = 16384
A_HEADS = 4
A_DK = 128
A_WIDTH = A_HEADS * A_DK
ATT_GROUPS = ((128, 1), (512, 4), (2048, 16))
N_GROUPS = 3
G_HEADS = 4
HEAD_DIM = 128
ATT_WIDTH = N_GROUPS * G_HEADS * HEAD_DIM
ATT_BLOCK = 128
ROT_DIM = HEAD_DIM // 4
ROPE_THETA = 500000.0
IN_COLS = 4 * A_WIDTH + 3 * ATT_WIDTH + 2 * D_MODEL
P_HEADS = 8
N_KEYS = 128
P_QDIM = 256
P_HALF = P_QDIM // 2
P_TOPK = 16
P_PAIRS = P_HEADS * P_TOPK
EPS = 1e-6

LANES = 128
SUBLANES = 8
HGRN_SUB = 16
ATT_TILE = ATT_BLOCK * ATT_GROUPS[-1][1]
VMEM_LIMIT = 56 * 1024 * 1024

NT_DIMS = (((1,), (1,)), ((), ()))
TN_DIMS = (((0,), (0,)), ((), ()))


def _cparams(*sem):
    return pltpu.CompilerParams(dimension_semantics=sem, vmem_limit_bytes=VMEM_LIMIT)


def _rms(x):
    return x * lax.rsqrt(jnp.mean(x * x, axis=-1, keepdims=True) + EPS)


def _sigmoid(x):
    return 1.0 / (1.0 + jnp.exp(-x))


def _mod_spec(arr, rows_per_group, tm, ngrid):
    if arr.ndim == 3:
        if ngrid == 2:
            return pl.BlockSpec((None, 1, arr.shape[-1]), lambda i, j: (i * tm // rows_per_group, 0, 0))
        return pl.BlockSpec((None, 1, arr.shape[-1]), lambda i: (i * tm // rows_per_group, 0, 0))
    if ngrid == 2:
        return pl.BlockSpec((tm, arr.shape[-1]), lambda i, j: (i, 0))
    return pl.BlockSpec((tm, arr.shape[-1]), lambda i: (i, 0))


def _ada_kernel(c_ref, w_ref, b_ref, o_ref):
    c = c_ref[...]
    s = c * _sigmoid(c)
    o_ref[...] = jnp.dot(s, w_ref[...], preferred_element_type=F32) + b_ref[...]


def _ada(c, w, b):
    rows, d = c.shape
    n = -(-rows // SUBLANES) * SUBLANES
    c = jnp.pad(c, ((0, n - rows), (0, 0)))
    cols = w.shape[1]
    tn = 1024
    out = pl.pallas_call(
        _ada_kernel,
        grid=(cols // tn,),
        in_specs=[pl.BlockSpec((n, d), lambda j: (0, 0)),
                  pl.BlockSpec((d, tn), lambda j: (0, j)),
                  pl.BlockSpec((1, tn), lambda j: (0, j))],
        out_specs=pl.BlockSpec((n, tn), lambda j: (0, j)),
        out_shape=jax.ShapeDtypeStruct((n, cols), F32),
        compiler_params=_cparams("arbitrary"),
        name="ada",
    )(c, w, b.reshape(1, cols))
    return out[:rows]


def _in_proj_kernel(x_ref, sc_ref, sh_ref, nw_ref, w_ref, o_ref, xn_ref):
    @pl.when(pl.program_id(1) == 0)
    def _():
        y = _rms(x_ref[...]) * nw_ref[...]
        y = y * (1.0 + sc_ref[...]) + sh_ref[...]
        xn_ref[...] = y.astype(BF16)

    o_ref[...] = jnp.dot(xn_ref[...], w_ref[...], preferred_element_type=F32)


def _in_proj(x2, sc, sh, nw, w_bf, rows_per_group):
    n, d = x2.shape
    cols = w_bf.shape[1]
    tm = min(1024, n, rows_per_group if sc.ndim == 3 else n)
    tn = cols // 4
    assert tn % LANES == 0
    return pl.pallas_call(
        _in_proj_kernel,
        grid=(n // tm, cols // tn),
        in_specs=[pl.BlockSpec((tm, d), lambda i, j: (i, 0)),
                  _mod_spec(sc, rows_per_group, tm, 2),
                  _mod_spec(sh, rows_per_group, tm, 2),
                  pl.BlockSpec((1, d), lambda i, j: (0, 0)),
                  pl.BlockSpec((d, tn), lambda i, j: (0, j))],
        out_specs=pl.BlockSpec((tm, tn), lambda i, j: (i, j)),
        out_shape=jax.ShapeDtypeStruct((n, cols), F32),
        scratch_shapes=[pltpu.VMEM((tm, d), BF16)],
        compiler_params=_cparams("parallel", "arbitrary"),
        name="in_proj",
    )(x2, sc, sh, nw.reshape(1, d), w_bf)


def _hgrn_kernel(q_ref, f_ref, i_ref, g_ref, lb_ref, s0_ref, nw_ref, o_ref, sout_ref,
                 st_ref, b_ref, k_ref, *, tt, sub, t_valid):
    t = pl.program_id(1)

    @pl.when(t == 0)
    def _():
        for h in range(A_HEADS):
            st_ref[h] = s0_ref[h].T

    lb = lb_ref[...]
    f = lb + (1.0 - lb) * _sigmoid(f_ref[...])
    logf = jnp.log(f)
    kk = 1.0 - f
    if t_valid is not None:
        live = lax.broadcasted_iota(jnp.int32, (tt, 1), 0) < t_valid
        logf = jnp.where(live, logf, 0.0)
        kk = jnp.where(live, kk, 0.0)
    r = lax.broadcasted_iota(jnp.int32, (tt, tt), 0)
    c = lax.broadcasted_iota(jnp.int32, (tt, tt), 1)
    tri = ((r // sub == c // sub) & (c <= r)).astype(F32)
    b_ref[...] = jnp.dot(tri, logf, preferred_element_type=F32, precision=lax.Precision.HIGHEST)
    k_ref[...] = kk

    row = lax.broadcasted_iota(jnp.int32, (sub, 1), 0)
    nw = nw_ref[...]

    def body(i, carry):
        sl = pl.ds(pl.multiple_of(i * sub, sub), sub)
        for h in range(A_HEADS):
            cols = slice(h * A_DK, (h + 1) * A_DK)
            bq = b_ref[sl, cols]
            kq = k_ref[sl, cols]
            qq = q_ref[sl, cols]
            vv = i_ref[sl, cols]
            st = st_ref[h]
            qe = qq * jnp.exp(bq)
            o = lax.dot_general(qe.astype(BF16), st.astype(BF16), NT_DIMS, preferred_element_type=F32)
            for s in range(sub):
                dd = jnp.where(row >= s, bq - bq[s:s + 1, :], -jnp.inf)
                m = qq * kq[s:s + 1, :] * jnp.exp(dd)
                o = o + jnp.sum(m, axis=-1, keepdims=True) * vv[s:s + 1, :]
            bl = bq[sub - 1:sub, :]
            kd = kq * jnp.exp(bl - bq)
            st_ref[h] = st * jnp.exp(bl) + lax.dot_general(vv.astype(BF16), kd.astype(BF16), TN_DIMS,
                                                           preferred_element_type=F32)
            gq = g_ref[sl, cols]
            o_ref[sl, cols] = _rms(o) * nw * (gq * _sigmoid(gq))
        return carry

    lax.fori_loop(0, tt // sub, body, 0)

    @pl.when(t == pl.num_programs(1) - 1)
    def _():
        for h in range(A_HEADS):
            sout_ref[h] = st_ref[h].T


def _hgrn(h3, lb, s0, nw, t_valid=None):
    b, t, _ = h3.shape
    tt = min(256, t)
    sub = min(HGRN_SUB, tt)
    blk = lambda off: pl.BlockSpec((None, tt, A_WIDTH), lambda bi, ti: (bi, ti, off))
    return pl.pallas_call(
        functools.partial(_hgrn_kernel, tt=tt, sub=sub, t_valid=t_valid),
        grid=(b, t // tt),
        in_specs=[blk(0), blk(1), blk(2), blk(3),
                  pl.BlockSpec((1, A_WIDTH), lambda bi, ti: (0, 0)),
                  pl.BlockSpec((None, A_HEADS, A_DK, A_DK), lambda bi, ti: (bi, 0, 0, 0)),
                  pl.BlockSpec((1, A_DK), lambda bi, ti: (0, 0))],
        out_specs=[pl.BlockSpec((None, tt, A_WIDTH), lambda bi, ti: (bi, ti, 0)),
                   pl.BlockSpec((None, A_HEADS, A_DK, A_DK), lambda bi, ti: (bi, 0, 0, 0))],
        out_shape=[jax.ShapeDtypeStruct((b, t, A_WIDTH), F32),
                   jax.ShapeDtypeStruct((b, A_HEADS, A_DK, A_DK), F32)],
        scratch_shapes=[pltpu.VMEM((A_HEADS, A_DK, A_DK), F32), pltpu.VMEM((tt, A_WIDTH), F32),
                        pltpu.VMEM((tt, A_WIDTH), F32)],
        compiler_params=_cparams("parallel", "arbitrary"),
        name="hgrn",
    )(h3, h3, h3, h3, lb.reshape(1, A_WIDTH), s0, nw.reshape(1, A_DK))


ROPE_HEADS = 4


def _qk_rope_kernel(h_ref, w_ref, cos_ref, sin_ref, o_ref):
    half = ROT_DIM // 2
    lane = lax.broadcasted_iota(jnp.int32, cos_ref.shape, 1)
    for k in range(ROPE_HEADS):
        cols = slice(k * HEAD_DIM, (k + 1) * HEAD_DIM)
        y = _rms(h_ref[:, cols]) * w_ref[...]
        partner = jnp.where(lane < half, pltpu.roll(y, LANES - half, axis=1), pltpu.roll(y, half, axis=1))
        o_ref[:, cols] = y * cos_ref[...] + partner * sin_ref[...]


def _rope_tables(pos):
    inv = ROPE_THETA ** (-jnp.arange(0, ROT_DIM, 2, dtype=F32) / ROT_DIM)
    ang = pos.astype(F32)[:, None] * inv[None, :]
    cos, sin = jnp.cos(ang), jnp.sin(ang)
    rest = HEAD_DIM - ROT_DIM
    cos_t = jnp.concatenate([cos, cos, jnp.ones((pos.shape[0], rest), F32)], axis=-1)
    sin_t = jnp.concatenate([-sin, sin, jnp.zeros((pos.shape[0], rest), F32)], axis=-1)
    return cos_t, sin_t


def _qk_rope(h3, qk_w, cos_t, sin_t):
    b, t, _ = h3.shape
    tt = min(1024, t)
    width = ROPE_HEADS * HEAD_DIM
    nq = ATT_WIDTH // width
    off = 4 * A_WIDTH // width
    return pl.pallas_call(
        _qk_rope_kernel,
        grid=(b, t // tt, 2 * nq),
        in_specs=[pl.BlockSpec((None, tt, width), lambda bi, ti, hi: (bi, ti, off + hi)),
                  pl.BlockSpec((None, 1, HEAD_DIM), lambda bi, ti, hi: (hi // nq, 0, 0)),
                  pl.BlockSpec((tt, HEAD_DIM), lambda bi, ti, hi: (ti, 0)),
                  pl.BlockSpec((tt, HEAD_DIM), lambda bi, ti, hi: (ti, 0))],
        out_specs=pl.BlockSpec((None, tt, width), lambda bi, ti, hi: (bi, ti, hi)),
        out_shape=jax.ShapeDtypeStruct((b, t, 2 * ATT_WIDTH), F32),
        compiler_params=_cparams("parallel", "parallel", "arbitrary"),
        name="qk_rope",
    )(h3, qk_w, cos_t, sin_t)


def _softmax_block(s_list, v_list):
    m = s_list[0].max(axis=-1, keepdims=True)
    for s in s_list[1:]:
        m = jnp.maximum(m, s.max(axis=-1, keepdims=True))
    den = None
    o = None
    for s, v in zip(s_list, v_list):
        p = jnp.exp(s - m)
        d = jnp.sum(p, axis=-1, keepdims=True)
        pv = jnp.dot(p.astype(BF16), v.astype(BF16), preferred_element_type=F32)
        den = d if den is None else den + d
        o = pv if o is None else o + pv
    return o / den, m + jnp.log(den)


def _attn_kernel(*refs, tq):
    ins = refs[:15]
    o_ref = refs[15]
    og_refs, lse_refs = refs[16:16 + N_GROUPS], refs[16 + N_GROUPS:16 + 2 * N_GROUPS]
    ti = pl.program_id(2)
    scale = HEAD_DIM ** -0.5
    qi = lax.broadcasted_iota(jnp.int32, (ATT_BLOCK, ATT_BLOCK), 0)
    ki = lax.broadcasted_iota(jnp.int32, (ATT_BLOCK, ATT_BLOCK), 1)
    for g, (win, dil) in enumerate(ATT_GROUPS):
        q_ref, kc_ref, kp_ref, vc_ref, vp_ref = ins[5 * g:5 * g + 5]
        span = ATT_BLOCK * dil
        nblk = tq // span

        def load(ref, start, dil=dil):
            if dil == 1:
                return ref[pl.ds(start, ATT_BLOCK), :]
            return ref[pl.ds(start, ATT_BLOCK, stride=dil), :]

        def body(i, carry, dil=dil, span=span, nblk=nblk, og_ref=og_refs[g], lse_ref=lse_refs[g],
                 q_ref=q_ref, kc_ref=kc_ref, kp_ref=kp_ref, vc_ref=vc_ref, vp_ref=vp_ref, load=load):
            r = i // nblk
            n = i % nblk
            start = r + span * n
            qb = load(q_ref, start).astype(BF16)
            kc = load(kc_ref, start)
            vc = load(vc_ref, start)
            last = r + span * (nblk - 1)
            if nblk == 1:
                kp = load(kp_ref, last)
                vp = load(vp_ref, last)
                has_prev = ti > 0
            else:
                inner = jnp.maximum(start - span, r)
                first = n == 0
                kp = jnp.where(first, load(kp_ref, last), load(kc_ref, inner))
                vp = jnp.where(first, load(vp_ref, last), load(vc_ref, inner))
                has_prev = jnp.logical_or(ti > 0, n > 0)
            s_c = lax.dot_general(qb, kc.astype(BF16), NT_DIMS, preferred_element_type=F32) * scale
            s_p = lax.dot_general(qb, kp.astype(BF16), NT_DIMS, preferred_element_type=F32) * scale
            s_c = jnp.where(ki <= qi, s_c, -jnp.inf)
            s_p = jnp.where(jnp.logical_and(ki >= qi, has_prev), s_p, -jnp.inf)
            o, lse = _softmax_block([s_c, s_p], [vc, vp])
            if dil == 1:
                og_ref[pl.ds(start, ATT_BLOCK), :] = o
                lse_ref[pl.ds(start, ATT_BLOCK), :] = jnp.broadcast_to(lse, o.shape)
            else:
                og_ref[pl.ds(start, ATT_BLOCK, stride=dil), :] = o
                lse_ref[pl.ds(start, ATT_BLOCK, stride=dil), :] = jnp.broadcast_to(lse, o.shape)
            return carry

        lax.fori_loop(0, tq // ATT_BLOCK, body, 0)

    l0, l1, l2 = lse_refs[0][...], lse_refs[1][...], lse_refs[2][...]
    m = jnp.maximum(jnp.maximum(l0, l1), l2)
    e0, e1, e2 = jnp.exp(l0 - m), jnp.exp(l1 - m), jnp.exp(l2 - m)
    z = e0 + e1 + e2
    o_ref[...] = (e0 / z) * og_refs[0][...] + (e1 / z) * og_refs[1][...] + (e2 / z) * og_refs[2][...]


def _attn_prompt(qk3, h3):
    b, t, _ = qk3.shape
    tq = ATT_TILE
    assert t % tq == 0
    nh = N_GROUPS * G_HEADS
    voff = (4 * A_WIDTH + 2 * ATT_WIDTH) // HEAD_DIM
    cur = lambda off: pl.BlockSpec((None, tq, HEAD_DIM), lambda bi, ji, ti, off=off: (bi, ti, off + ji))
    prev = lambda off: pl.BlockSpec((None, tq, HEAD_DIM),
                                    lambda bi, ji, ti, off=off: (bi, jnp.maximum(ti - 1, 0), off + ji))
    in_specs, args = [], []
    for g in range(N_GROUPS):
        in_specs += [cur(g * G_HEADS), cur(nh + g * G_HEADS), prev(nh + g * G_HEADS),
                     cur(voff + g * G_HEADS), prev(voff + g * G_HEADS)]
        args += [qk3, qk3, qk3, h3, h3]
    return pl.pallas_call(
        functools.partial(_attn_kernel, tq=tq),
        grid=(b, G_HEADS, t // tq),
        in_specs=in_specs,
        out_specs=pl.BlockSpec((None, tq, HEAD_DIM), lambda bi, ji, ti: (bi, ti, ji)),
        out_shape=jax.ShapeDtypeStruct((b, t, G_HEADS * HEAD_DIM), F32),
        scratch_shapes=[pltpu.VMEM((tq, HEAD_DIM), F32)] * (2 * N_GROUPS),
        compiler_params=_cparams("parallel", "parallel", "arbitrary"),
        name="attn_prompt",
    )(*args)


def _attn_sample_kernel(*refs, tp, t_valid):
    o_ref = refs[12]
    scale = HEAD_DIM ** -0.5
    outs, lses = [], []
    for g, (win, dil) in enumerate(ATT_GROUPS):
        c_ref, q_ref, k_ref, v_ref = refs[4 * g:4 * g + 4]
        n_back = win // dil
        w = c_ref.shape[1]
        qb = q_ref[...].astype(BF16)
        s_b = lax.dot_general(qb, c_ref[0].astype(BF16), NT_DIMS, preferred_element_type=F32) * scale
        s_n = lax.dot_general(qb, k_ref[...].astype(BF16), NT_DIMS, preferred_element_type=F32) * scale
        tq_b = lax.broadcasted_iota(jnp.int32, (tp, w), 0)
        rb = lax.broadcasted_iota(jnp.int32, (tp, w), 1)
        delta = w + tq_b - rb
        ok_b = jnp.logical_and(delta % dil == 0, delta // dil <= n_back)
        tq_n = lax.broadcasted_iota(jnp.int32, (tp, tp), 0)
        tn = lax.broadcasted_iota(jnp.int32, (tp, tp), 1)
        dn = tq_n - tn
        ok_n = (dn >= 0) & (dn % dil == 0) & (dn // dil <= n_back) & (tn < t_valid)
        s_b = jnp.where(ok_b, s_b, -jnp.inf)
        s_n = jnp.where(ok_n, s_n, -jnp.inf)
        o, lse = _softmax_block([s_b, s_n], [c_ref[1], v_ref[...]])
        outs.append(o)
        lses.append(lse)
    m = jnp.maximum(jnp.maximum(lses[0], lses[1]), lses[2])
    es = [jnp.exp(l - m) for l in lses]
    z = es[0] + es[1] + es[2]
    o_ref[...] = (es[0] / z) * outs[0] + (es[1] / z) * outs[1] + (es[2] / z) * outs[2]


def _attn_sample(qk3, h3, caches, t_valid):
    b, tp, _ = qk3.shape
    nh = N_GROUPS * G_HEADS
    voff = (4 * A_WIDTH + 2 * ATT_WIDTH) // HEAD_DIM
    row = lambda off: pl.BlockSpec((None, tp, HEAD_DIM), lambda bi, ji, off=off: (bi, 0, off + ji))
    in_specs, args = [], []
    for g in range(N_GROUPS):
        w = caches[g].shape[3]
        in_specs += [pl.BlockSpec((None, 2, None, w, HEAD_DIM), lambda bi, ji: (bi, 0, ji, 0, 0)),
                     row(g * G_HEADS), row(nh + g * G_HEADS), row(voff + g * G_HEADS)]
        args += [caches[g], qk3, qk3, h3]
    return pl.pallas_call(
        functools.partial(_attn_sample_kernel, tp=tp, t_valid=t_valid),
        grid=(b, G_HEADS),
        in_specs=in_specs,
        out_specs=pl.BlockSpec((None, tp, HEAD_DIM), lambda bi, ji: (bi, 0, ji)),
        out_shape=jax.ShapeDtypeStruct((b, tp, G_HEADS * HEAD_DIM), F32),
        compiler_params=_cparams("parallel", "arbitrary"),
        name="attn_sample",
    )(*args)


def _out_proj_kernel(oa_ref, ob_ref, ga0_ref, ga1_ref, gb0_ref, gb1_ref, x_ref, g1_ref, sc_ref, sh_ref,
                     nw_ref, wa_ref, wb_ref, wo_ref, x1_ref, n2_ref):
    ya = jnp.dot(oa_ref[...].astype(BF16), wa_ref[...], preferred_element_type=F32)
    yb = jnp.dot(ob_ref[...].astype(BF16), wb_ref[...], preferred_element_type=F32)
    gate_a = _sigmoid(jnp.concatenate([ga0_ref[...], ga1_ref[...]], axis=-1))
    gate_b = _sigmoid(jnp.concatenate([gb0_ref[...], gb1_ref[...]], axis=-1))
    mix = jnp.dot((gate_a * ya + gate_b * yb).astype(BF16), wo_ref[...], preferred_element_type=F32)
    x1 = x_ref[...] + g1_ref[...] * mix
    x1_ref[...] = x1
    n2_ref[...] = (_rms(x1) * nw_ref[...]) * (1.0 + sc_ref[...]) + sh_ref[...]


def _out_proj(oa2, ob2, h2, x2, g1, sc2, sh2, nw2, wa_bf, wb_bf, wo_bf, rows_per_group):
    n, d = x2.shape
    tm = min(512, n, rows_per_group if g1.ndim == 3 else n)
    gw = 512
    goff = (4 * A_WIDTH + 3 * ATT_WIDTH) // gw
    rows = lambda width: pl.BlockSpec((tm, width), lambda i: (i, 0))
    gcol = lambda k: pl.BlockSpec((tm, gw), lambda i, k=k: (i, goff + k))
    full = lambda a: pl.BlockSpec(a.shape, lambda i: (0, 0))
    mod = lambda a: _mod_spec(a, rows_per_group, tm, 1)
    return pl.pallas_call(
        _out_proj_kernel,
        grid=(n // tm,),
        in_specs=[rows(A_WIDTH), rows(G_HEADS * HEAD_DIM), gcol(0), gcol(1), gcol(2), gcol(3), rows(d),
                  mod(g1), mod(sc2), mod(sh2), pl.BlockSpec((1, d), lambda i: (0, 0)),
                  full(wa_bf), full(wb_bf), full(wo_bf)],
        out_specs=[rows(d), rows(d)],
        out_shape=[jax.ShapeDtypeStruct((n, d), F32), jax.ShapeDtypeStruct((n, d), F32)],
        compiler_params=_cparams("parallel"),
        name="out_proj",
    )(oa2, ob2, h2, h2, h2, h2, x2, g1, sc2, sh2, nw2.reshape(1, d), wa_bf, wb_bf, wo_bf)


def _topk_rows(s, k, extra=None):
    rows = s.shape[0]
    ridx = lax.broadcasted_iota(jnp.int32, s.shape, 0).astype(F32)
    vals, idxs, extras = [], [], []
    for _ in range(k):
        m = jnp.max(s, axis=0, keepdims=True)
        idx = jnp.min(jnp.where(s == m, ridx, float(rows)), axis=0, keepdims=True)
        hit = ridx == idx
        vals.append(m)
        idxs.append(idx.astype(jnp.int32))
        if extra is not None:
            extras.append(jnp.sum(jnp.where(hit, extra, 0), axis=0, keepdims=True))
        s = jnp.where(hit, -jnp.inf, s)
    return vals, idxs, extras


CAND_CELLS = tuple((a, b) for a in range(P_TOPK) for b in range(P_TOPK) if (a + 1) * (b + 1) <= P_TOPK)
CAND_PAD = -len(CAND_CELLS) % SUBLANES
SEL_TILE = 128


def _peer_sel_kernel(n2_ref, wq_ref, qn_ref, k1_ref, k2_ref, e_ref, g_ref, q_ref, *, tm):
    q = jnp.dot(n2_ref[...].astype(BF16), wq_ref[...], preferred_element_type=F32)
    q_ref[...] = _rms(q) * qn_ref[...]
    hi = lax.Precision.HIGHEST

    def tile(ti, carry):
        q = q_ref[pl.ds(pl.multiple_of(ti * SEL_TILE, SEL_TILE), SEL_TILE), :]
        s1 = lax.dot_general(k1_ref[...], q[:, :P_HALF], NT_DIMS, preferred_element_type=F32, precision=hi)
        s2 = lax.dot_general(k2_ref[...], q[:, P_HALF:], NT_DIMS, preferred_element_type=F32, precision=hi)
        v1, i1, _ = _topk_rows(s1, P_TOPK)
        v2, i2, _ = _topk_rows(s2, P_TOPK)
        cand = [v1[a] + v2[b] for a, b in CAND_CELLS]
        cidx = [i1[a] * N_KEYS + i2[b] for a, b in CAND_CELLS]
        cand += [jnp.full_like(cand[0], -jnp.inf)] * CAND_PAD
        cidx += [jnp.zeros_like(cidx[0])] * CAND_PAD
        sc, _, eidx = _topk_rows(jnp.concatenate(cand, axis=0), P_TOPK, extra=jnp.concatenate(cidx, axis=0))
        sc = jnp.concatenate(sc, axis=0)
        p = jnp.exp(sc - jnp.max(sc, axis=0, keepdims=True))
        cols = pl.ds(pl.multiple_of(ti * SEL_TILE, SEL_TILE), SEL_TILE)
        g_ref[:, cols] = p / jnp.sum(p, axis=0, keepdims=True)
        e_ref[:, cols] = jnp.concatenate(eidx, axis=0)
        return carry

    lax.fori_loop(0, tm // SEL_TILE, tile, 0)


def _peer_sel(n2, wq_bf, qn_w, k1, k2):
    n, d = n2.shape
    tm = min(256, n)
    assert tm % SEL_TILE == 0
    return pl.pallas_call(
        functools.partial(_peer_sel_kernel, tm=tm),
        grid=(n // tm, P_HEADS),
        in_specs=[pl.BlockSpec((tm, d), lambda i, h: (i, 0)),
                  pl.BlockSpec((d, P_QDIM), lambda i, h: (0, h)),
                  pl.BlockSpec((1, P_QDIM), lambda i, h: (0, 0)),
                  pl.BlockSpec((None, N_KEYS, P_HALF), lambda i, h: (h, 0, 0)),
                  pl.BlockSpec((None, N_KEYS, P_HALF), lambda i, h: (h, 0, 0))],
        out_specs=[pl.BlockSpec((None, P_TOPK, tm), lambda i, h: (h, 0, i)),
                   pl.BlockSpec((None, P_TOPK, tm), lambda i, h: (h, 0, i))],
        out_shape=[jax.ShapeDtypeStruct((P_HEADS, P_TOPK, n), jnp.int32),
                   jax.ShapeDtypeStruct((P_HEADS, P_TOPK, n), F32)],
        scratch_shapes=[pltpu.VMEM((tm, P_QDIM), F32)],
        compiler_params=_cparams("parallel", "arbitrary"),
        name="peer_sel",
    )(n2, wq_bf, qn_w.reshape(1, P_QDIM), k1, k2)


def _pack_table(tab):
    e, d = tab.shape
    bits = lax.bitcast_convert_type(tab.astype(BF16), jnp.uint16).astype(jnp.uint32)
    bits = bits.reshape(e // 2, 2, d // LANES, LANES)
    return (bits[:, 0] | (bits[:, 1] << 16)).reshape(e // 2 * (d // LANES), LANES)


HI_HALF = 0xFFFF0000
EVEN_BIT = 4


def _index_word(e):
    return ((e >> 1) * SUBLANES) | ((1 - (e & 1)) * EVEN_BIT)


def _expert_row(tab_ref, word):
    tile = tab_ref[pl.ds(pl.multiple_of(word & ~(SUBLANES - 1), SUBLANES), SUBLANES), :]
    wv = jnp.full((SUBLANES, LANES), word, jnp.int32).astype(jnp.uint32)
    sh = (wv & jnp.uint32(EVEN_BIT)) << jnp.uint32(2)
    return lax.bitcast_convert_type((tile << sh) & jnp.uint32(HI_HALF), F32)


def _gelu(x):
    return 0.5 * x * (1.0 + jnp.tanh(0.7978845608028654 * (x + 0.044715 * x * x * x)))


def _smem_block_copies(srcs, dsts, sem, tb):
    rows = pl.ds(pl.program_id(0) * (tb * P_PAIRS), tb * P_PAIRS)
    return [pltpu.make_async_copy(src.at[rows], dst, sem.at[k]) for k, (src, dst) in enumerate(zip(srcs, dsts))]


PAIR_CHUNK = 32
N_CHUNKS = P_PAIRS // PAIR_CHUNK
U_CHUNK = 32
U_CHUNKS = P_PAIRS // U_CHUNK
U_FOLD = SUBLANES // U_CHUNKS
U_GROUP = 8


def _peer_u_row(tab_ref, idx_smem, x_ref, t, fill_ref, sum_ref):
    x_t = x_ref[t]

    def chunk(c, part):
        q0 = c * U_CHUNK
        base = t * P_PAIRS + q0
        for p in range(U_CHUNK):
            row = _expert_row(tab_ref, idx_smem[base + p])
            fill_ref[pl.ds(pl.multiple_of((q0 + p) * SUBLANES, SUBLANES), SUBLANES), :] = row * x_t
        for k in range(U_FOLD):
            part = part + sum_ref[pl.ds(c * U_FOLD + k, P_PAIRS, stride=SUBLANES), :]
        return part

    return lax.fori_loop(0, U_CHUNKS, chunk, jnp.zeros((P_PAIRS, LANES), F32))


def _peer_u_kernel(idx_hbm, x_ref, g_ref, tab_ref, w_ref, idx_smem, sem, stage_a, stage_b, part_ref, act_ref, *, tb):
    copies = _smem_block_copies((idx_hbm,), (idx_smem,), sem, tb)
    for cp in copies:
        cp.start()
    for cp in copies:
        cp.wait()
    stage_b[...] = jnp.zeros(stage_b.shape, F32)

    def two_rows(j, carry):
        t0 = 2 * j
        part_ref[jnp.maximum(t0 - 1, 0)] = _peer_u_row(tab_ref, idx_smem, x_ref, t0, stage_a, stage_b)
        part_ref[t0] = _peer_u_row(tab_ref, idx_smem, x_ref, t0 + 1, stage_b, stage_a)
        return carry

    lax.fori_loop(0, tb // 2, two_rows, 0)
    last = stage_b[pl.ds(0, P_PAIRS, stride=SUBLANES), :]
    for k in range(1, SUBLANES):
        last = last + stage_b[pl.ds(k, P_PAIRS, stride=SUBLANES), :]
    part_ref[tb - 1] = last

    eye = (lax.broadcasted_iota(jnp.int32, (P_PAIRS, LANES), 0)
           == lax.broadcasted_iota(jnp.int32, (P_PAIRS, LANES), 1))

    def lane_sums(gi, carry):
        rows = []
        for k in range(U_GROUP):
            col = jnp.sum(part_ref[gi * U_GROUP + k], axis=1, keepdims=True)
            rows.append(jnp.sum(jnp.where(eye, col, 0.0), axis=0, keepdims=True))
        act_ref[pl.ds(pl.multiple_of(gi * U_GROUP, U_GROUP), U_GROUP), :] = jnp.concatenate(rows, axis=0)
        return carry

    lax.fori_loop(0, tb // U_GROUP, lane_sums, 0)
    w_ref[...] = g_ref[...] * _gelu(act_ref[...])


def _peer_u(idx, x3, g2d, tab):
    n = x3.shape[0]
    tb = min(128, n)
    assert tb % U_GROUP == 0 and n % tb == 0
    return pl.pallas_call(
        functools.partial(_peer_u_kernel, tb=tb),
        grid=(n // tb,),
        in_specs=[pl.BlockSpec(memory_space=pl.ANY),
                  pl.BlockSpec((tb, SUBLANES, LANES), lambda i: (i, 0, 0)),
                  pl.BlockSpec((tb, P_PAIRS), lambda i: (i, 0)),
                  pl.BlockSpec(tab.shape, lambda i: (0, 0), pipeline_mode=pl.Buffered(1))],
        out_specs=pl.BlockSpec((tb, P_PAIRS), lambda i: (i, 0)),
        out_shape=jax.ShapeDtypeStruct((n, P_PAIRS), F32),
        scratch_shapes=[pltpu.SMEM((tb * P_PAIRS,), jnp.int32), pltpu.SemaphoreType.DMA((1,)),
                        pltpu.VMEM((P_PAIRS * SUBLANES, LANES), F32),
                        pltpu.VMEM((P_PAIRS * SUBLANES, LANES), F32),
                        pltpu.VMEM((tb, P_PAIRS, LANES), F32),
                        pltpu.VMEM((tb, P_PAIRS), F32)],
        compiler_params=_cparams("arbitrary"),
        name="peer_u",
    )(idx, x3, g2d, tab)


def _peer_v_kernel(idx_hbm, w_hbm, x1_ref, g2_ref, tab_ref, y_ref, idx_smem, w_smem, sem, out_ref, *, tb):
    copies = _smem_block_copies((idx_hbm, w_hbm), (idx_smem, w_smem), sem, tb)
    for cp in copies:
        cp.start()
    for cp in copies:
        cp.wait()
    nacc = 4

    def tok(t, carry):
        def chunk(c, accs):
            accs = list(accs)
            base = t * P_PAIRS + c * PAIR_CHUNK
            for p in range(PAIR_CHUNK):
                accs[p % nacc] = accs[p % nacc] + w_smem[base + p] * _expert_row(tab_ref, idx_smem[base + p])
            return tuple(accs)

        zero = jnp.zeros((SUBLANES, LANES), F32)
        accs = lax.fori_loop(0, N_CHUNKS, chunk, (zero,) * nacc)
        out_ref[t] = (accs[0] + accs[1]) + (accs[2] + accs[3])
        return carry

    lax.fori_loop(0, tb, tok, 0)
    y_ref[...] = x1_ref[...] + g2_ref[...] * out_ref[...]


def _peer_v(idx, w2d, x13, g2, tab, rows_per_group):
    n = x13.shape[0]
    tb = min(256, n, rows_per_group if g2.ndim == 4 else n)
    if g2.ndim == 4:
        g2_spec = pl.BlockSpec((None, 1, SUBLANES, LANES), lambda i: (i * tb // rows_per_group, 0, 0, 0))
    else:
        g2_spec = pl.BlockSpec((tb, SUBLANES, LANES), lambda i: (i, 0, 0))
    return pl.pallas_call(
        functools.partial(_peer_v_kernel, tb=tb),
        grid=(n // tb,),
        in_specs=[pl.BlockSpec(memory_space=pl.ANY),
                  pl.BlockSpec(memory_space=pl.ANY),
                  pl.BlockSpec((tb, SUBLANES, LANES), lambda i: (i, 0, 0)),
                  g2_spec,
                  pl.BlockSpec(tab.shape, lambda i: (0, 0), pipeline_mode=pl.Buffered(1))],
        out_specs=pl.BlockSpec((tb, SUBLANES, LANES), lambda i: (i, 0, 0)),
        out_shape=jax.ShapeDtypeStruct((n, SUBLANES, LANES), F32),
        scratch_shapes=[pltpu.SMEM((tb * P_PAIRS,), jnp.int32), pltpu.SMEM((tb * P_PAIRS,), F32),
                        pltpu.SemaphoreType.DMA((2,)), pltpu.VMEM((tb, SUBLANES, LANES), F32)],
        compiler_params=_cparams("arbitrary"),
        name="peer_v",
    )(idx, w2d, x13, g2, tab)


def _group_mod(m):
    return m[:, None, :]


def _layer(x, mod, pos, lb, s0, kv_bufs, wts):
    (norm1_w, norm2_w, w_in_bf, hgrn_norm_w, qk_w, wa_bf, wb_bf, wo_bf, wq_bf, peer_qn_w,
     peer_k1, peer_k2, u_tab, v_tab) = wts
    b, t, d = x.shape
    n = b * t
    sample = kv_bufs is not None
    sh1, sc1, g1, sh2, sc2, g2 = jnp.split(mod, 6, axis=-1)
    if sample:
        per_row = lambda m: jnp.repeat(m, t, axis=0)
        sh1, sc1, g1, sh2, sc2 = map(per_row, (sh1, sc1, g1, sh2, sc2))
        g2 = per_row(g2).reshape(n, SUBLANES, LANES)
    else:
        sh1, sc1, g1, sh2, sc2 = map(_group_mod, (sh1, sc1, g1, sh2, sc2))
        g2 = g2.reshape(b, 1, SUBLANES, LANES)
    x2 = x.reshape(n, d)
    h2 = _in_proj(x2, sc1, sh1, norm1_w, w_in_bf, t)
    h3 = h2.reshape(b, t, IN_COLS)

    if sample:
        tp = HGRN_SUB
        h3p = jnp.pad(h3, ((0, 0), (0, tp - t), (0, 0)))
        pos_p = pos[0] + jnp.arange(tp)
        oa, s_new = _hgrn(h3p, lb, s0, hgrn_norm_w, t_valid=t)
        cos_t, sin_t = _rope_tables(pos_p)
        qk3 = _qk_rope(h3p, qk_w, cos_t, sin_t)
        ob = _attn_sample(qk3, h3p, kv_bufs, t)
        oa, ob, qk3 = oa[:, :t], ob[:, :t], qk3[:, :t]
    else:
        oa, s_new = _hgrn(h3, lb, s0, hgrn_norm_w)
        cos_t, sin_t = _rope_tables(pos)
        qk3 = _qk_rope(h3, qk_w, cos_t, sin_t)
        ob = _attn_prompt(qk3, h3)

    kn = qk3[..., ATT_WIDTH:].reshape(b, t, N_GROUPS * G_HEADS, HEAD_DIM)
    voff = 4 * A_WIDTH + 2 * ATT_WIDTH
    vn = h3[..., voff:voff + ATT_WIDTH].reshape(b, t, N_GROUPS * G_HEADS, HEAD_DIM)
    new_kv = []
    for gi, (win, _) in enumerate(ATT_GROUPS):
        keep = t if sample else min(win, t)
        hs = slice(gi * G_HEADS, (gi + 1) * G_HEADS)
        kg = kn[:, t - keep:, hs].transpose(0, 2, 1, 3)
        vg = vn[:, t - keep:, hs].transpose(0, 2, 1, 3)
        new_kv.append(jnp.stack([kg, vg], axis=1))

    x1, n2 = _out_proj(oa.reshape(n, A_WIDTH), ob.reshape(n, G_HEADS * HEAD_DIM), h2, x2, g1, sc2, sh2,
                       norm2_w, wa_bf, wb_bf, wo_bf, t)
    e_t, g_t = _peer_sel(n2, wq_bf, peer_qn_w, peer_k1, peer_k2)
    idx = _index_word(e_t.reshape(P_PAIRS, n).T).reshape(n * P_PAIRS)
    g2d = g_t.reshape(P_PAIRS, n).T
    w = _peer_u(idx, n2.reshape(n, SUBLANES, LANES), g2d, u_tab)
    y3 = _peer_v(idx, w.reshape(n * P_PAIRS), x1.reshape(n, SUBLANES, LANES), g2, v_tab, t)
    return y3.reshape(b, t, d), new_kv, s_new


def kernel(x_prompt, x_sample, cache_kv_g1, cache_kv_g2, cache_kv_g3, state_hgrn, c_prompt, c_sample, w_ada, b_ada, norm1_w, norm2_w, w_in, lb_logits, hgrn_norm_w, q_norm_w, k_norm_w, w_branch_a, w_branch_b, w_out, peer_wq, peer_qn_w, peer_k1, peer_k2, peer_u, peer_v):
    depth = w_ada.shape[0]
    bp, tp_len, _ = x_prompt.shape
    bs, ts_len, _ = x_sample.shape
    pos_p = jnp.arange(tp_len)
    pos_s = PAST_LEN + jnp.arange(ts_len)
    lb_all = jnp.cumsum(jax.nn.softmax(lb_logits.astype(F32), axis=0), axis=0)
    caches = (cache_kv_g1, cache_kv_g2, cache_kv_g3)
    c_all = jnp.concatenate([c_prompt, c_sample], axis=0)
    yp, ys = x_prompt, x_sample
    kvp, kvs = ([], [], []), ([], [], [])
    sp_list, ss_list = [], []
    for l in range(depth):
        mod = _ada(c_all, w_ada[l], b_ada[l])
        wts = (norm1_w[l], norm2_w[l], w_in[l].astype(BF16), hgrn_norm_w[l],
               jnp.stack([q_norm_w[l], k_norm_w[l]])[:, None, :],
               w_branch_a[l].astype(BF16), w_branch_b[l].astype(BF16), w_out[l].astype(BF16),
               peer_wq[l].astype(BF16), peer_qn_w[l], peer_k1[l], peer_k2[l],
               _pack_table(peer_u[l]), _pack_table(peer_v[l]))
        lb = lb_all[l].reshape(A_HEADS, A_DK)
        s0p = jnp.zeros((bp, A_HEADS, A_DK, A_DK), F32)
        yp, nkv_p, sp = _layer(yp, mod[:bp], pos_p, lb, s0p, None, wts)
        ys, nkv_s, ss = _layer(ys, mod[bp:], pos_s, lb, state_hgrn[l],
                               tuple(c[l] for c in caches), wts)
        for gi in range(N_GROUPS):
            kvp[gi].append(nkv_p[gi])
            kvs[gi].append(nkv_s[gi])
        sp_list.append(sp)
        ss_list.append(ss)
    return (yp, ys, jnp.stack(kvp[0]), jnp.stack(kvp[1]), jnp.stack(kvp[2]), jnp.stack(sp_list),
            jnp.stack(kvs[0]), jnp.stack(kvs[1]), jnp.stack(kvs[2]), jnp.stack(ss_list))
```

```python
import functools

import jax
import jax.numpy as jnp
from jax import lax
from jax.experimental import pallas as pl
from jax.experimental.pallas import tpu as pltpu

F32 = jnp.float32
BF16 = jnp.bfloat16

D_MODEL = 1024
PAST_LEN = 16384
A_HEADS = 4
A_DK = 128
A_WIDTH = A_HEADS * A_DK
ATT_GROUPS = ((128, 1), (512, 4), (2048, 16))
N_GROUPS = 3
G_HEADS = 4
HEAD_DIM = 128
ATT_WIDTH = N_GROUPS * G_HEADS * HEAD_DIM
ATT_BLOCK = 128
ROT_DIM = HEAD_DIM // 4
ROPE_THETA = 500000.0
IN_COLS = 4 * A_WIDTH + 3 * ATT_WIDTH + 2 * D_MODEL
P_HEADS = 8
N_KEYS = 128
P_QDIM = 256
P_HALF = P_QDIM // 2
P_TOPK = 16
P_PAIRS = P_HEADS * P_TOPK
EPS = 1e-6

LANES = 128
SUBLANES = 8
HGRN_SUB = 16
ATT_TILE = ATT_BLOCK * ATT_GROUPS[-1][1]
VMEM_LIMIT = 56 * 1024 * 1024

NT_DIMS = (((1,), (1,)), ((), ()))
TN_DIMS = (((0,), (0,)), ((), ()))


def _cparams(*sem):
    return pltpu.CompilerParams(dimension_semantics=sem, vmem_limit_bytes=VMEM_LIMIT)


def _rms(x):
    return x * lax.rsqrt(jnp.mean(x * x, axis=-1, keepdims=True) + EPS)


def _sigmoid(x):
    return 1.0 / (1.0 + jnp.exp(-x))


def _mod_spec(arr, rows_per_group, tm, ngrid):
    if arr.ndim == 3:
        if ngrid == 2:
            return pl.BlockSpec((None, 1, arr.shape[-1]), lambda i, j: (i * tm // rows_per_group, 0, 0))
        return pl.BlockSpec((None, 1, arr.shape[-1]), lambda i: (i * tm // rows_per_group, 0, 0))
    if ngrid == 2:
        return pl.BlockSpec((tm, arr.shape[-1]), lambda i, j: (i, 0))
    return pl.BlockSpec((tm, arr.shape[-1]), lambda i: (i, 0))


def _ada_kernel(c_ref, w_ref, b_ref, o_ref):
    c = c_ref[...]
    s = c * _sigmoid(c)
    o_ref[...] = jnp.dot(s, w_ref[...], preferred_element_type=F32) + b_ref[...]


def _ada(c, w, b):
    rows, d = c.shape
    n = -(-rows // SUBLANES) * SUBLANES
    c = jnp.pad(c, ((0, n - rows), (0, 0)))
    cols = w.shape[1]
    tn = 1024
    out = pl.pallas_call(
        _ada_kernel,
        grid=(cols // tn,),
        in_specs=[pl.BlockSpec((n, d), lambda j: (0, 0)),
                  pl.BlockSpec((d, tn), lambda j: (0, j)),
                  pl.BlockSpec((1, tn), lambda j: (0, j))],
        out_specs=pl.BlockSpec((n, tn), lambda j: (0, j)),
        out_shape=jax.ShapeDtypeStruct((n, cols), F32),
        compiler_params=_cparams("arbitrary"),
        name="ada",
    )(c, w, b.reshape(1, cols))
    return out[:rows]


def _in_proj_kernel(x_ref, sc_ref, sh_ref, nw_ref, w_ref, o_ref, xn_ref):
    @pl.when(pl.program_id(1) == 0)
    def _():
        y = _rms(x_ref[...]) * nw_ref[...]
        y = y * (1.0 + sc_ref[...]) + sh_ref[...]
        xn_ref[...] = y.astype(BF16)

    o_ref[...] = jnp.dot(xn_ref[...], w_ref[...], preferred_element_type=F32)


def _in_proj(x2, sc, sh, nw, w_bf, rows_per_group):
    n, d = x2.shape
    cols = w_bf.shape[1]
    tm = min(1024, n, rows_per_group if sc.ndim == 3 else n)
    tn = cols // 4
    assert tn % LANES == 0
    return pl.pallas_call(
        _in_proj_kernel,
        grid=(n // tm, cols // tn),
        in_specs=[pl.BlockSpec((tm, d), lambda i, j: (i, 0)),
                  _mod_spec(sc, rows_per_group, tm, 2),
                  _mod_spec(sh, rows_per_group, tm, 2),
                  pl.BlockSpec((1, d), lambda i, j: (0, 0)),
                  pl.BlockSpec((d, tn), lambda i, j: (0, j))],
        out_specs=pl.BlockSpec((tm, tn), lambda i, j: (i, j)),
        out_shape=jax.ShapeDtypeStruct((n, cols), F32),
        scratch_shapes=[pltpu.VMEM((tm, d), BF16)],
        compiler_params=_cparams("parallel", "arbitrary"),
        name="in_proj",
    )(x2, sc, sh, nw.reshape(1, d), w_bf)


def _hgrn_kernel(q_ref, f_ref, i_ref, g_ref, lb_ref, s0_ref, nw_ref, o_ref, sout_ref,
                 st_ref, b_ref, k_ref, *, tt, sub, t_valid):
    t = pl.program_id(1)

    @pl.when(t == 0)
    def _():
        for h in range(A_HEADS):
            st_ref[h] = s0_ref[h].T

    lb = lb_ref[...]
    f = lb + (1.0 - lb) * _sigmoid(f_ref[...])
    logf = jnp.log(f)
    kk = 1.0 - f
    if t_valid is not None:
        live = lax.broadcasted_iota(jnp.int32, (tt, 1), 0) < t_valid
        logf = jnp.where(live, logf, 0.0)
        kk = jnp.where(live, kk, 0.0)
    r = lax.broadcasted_iota(jnp.int32, (tt, tt), 0)
    c = lax.broadcasted_iota(jnp.int32, (tt, tt), 1)
    tri = ((r // sub == c // sub) & (c <= r)).astype(F32)
    b_ref[...] = jnp.dot(tri, logf, preferred_element_type=F32, precision=lax.Precision.HIGHEST)
    k_ref[...] = kk

    row = lax.broadcasted_iota(jnp.int32, (sub, 1), 0)
    nw = nw_ref[...]

    def body(i, carry):
        sl = pl.ds(pl.multiple_of(i * sub, sub), sub)
        for h in range(A_HEADS):
            cols = slice(h * A_DK, (h + 1) * A_DK)
            bq = b_ref[sl, cols]
            kq = k_ref[sl, cols]
            qq = q_ref[sl, cols]
            vv = i_ref[sl, cols]
            st = st_ref[h]
            qe = qq * jnp.exp(bq)
            o = lax.dot_general(qe.astype(BF16), st.astype(BF16), NT_DIMS, preferred_element_type=F32)
            for s in range(sub):
                dd = jnp.where(row >= s, bq - bq[s:s + 1, :], -jnp.inf)
                m = qq * kq[s:s + 1, :] * jnp.exp(dd)
                o = o + jnp.sum(m, axis=-1, keepdims=True) * vv[s:s + 1, :]
            bl = bq[sub - 1:sub, :]
            kd = kq * jnp.exp(bl - bq)
            st_ref[h] = st * jnp.exp(bl) + lax.dot_general(vv.astype(BF16), kd.astype(BF16), TN_DIMS,
                                                           preferred_element_type=F32)
            gq = g_ref[sl, cols]
            o_ref[sl, cols] = _rms(o) * nw * (gq * _sigmoid(gq))
        return carry

    lax.fori_loop(0, tt // sub, body, 0)

    @pl.when(t == pl.num_programs(1) - 1)
    def _():
        for h in range(A_HEADS):
            sout_ref[h] = st_ref[h].T


def _hgrn(h3, lb, s0, nw, t_valid=None):
    b, t, _ = h3.shape
    tt = min(256, t)
    sub = min(HGRN_SUB, tt)
    blk = lambda off: pl.BlockSpec((None, tt, A_WIDTH), lambda bi, ti: (bi, ti, off))
    return pl.pallas_call(
        functools.partial(_hgrn_kernel, tt=tt, sub=sub, t_valid=t_valid),
        grid=(b, t // tt),
        in_specs=[blk(0), blk(1), blk(2), blk(3),
                  pl.BlockSpec((1, A_WIDTH), lambda bi, ti: (0, 0)),
                  pl.BlockSpec((None, A_HEADS, A_DK, A_DK), lambda bi, ti: (bi, 0, 0, 0)),
                  pl.BlockSpec((1, A_DK), lambda bi, ti: (0, 0))],
        out_specs=[pl.BlockSpec((None, tt, A_WIDTH), lambda bi, ti: (bi, ti, 0)),
                   pl.BlockSpec((None, A_HEADS, A_DK, A_DK), lambda bi, ti: (bi, 0, 0, 0))],
        out_shape=[jax.ShapeDtypeStruct((b, t, A_WIDTH), F32),
                   jax.ShapeDtypeStruct((b, A_HEADS, A_DK, A_DK), F32)],
        scratch_shapes=[pltpu.VMEM((A_HEADS, A_DK, A_DK), F32), pltpu.VMEM((tt, A_WIDTH), F32),
                        pltpu.VMEM((tt, A_WIDTH), F32)],
        compiler_params=_cparams("parallel", "arbitrary"),
        name="hgrn",
    )(h3, h3, h3, h3, lb.reshape(1, A_WIDTH), s0, nw.reshape(1, A_DK))


ROPE_HEADS = 4


def _qk_rope_kernel(h_ref, w_ref, cos_ref, sin_ref, o_ref):
    half = ROT_DIM // 2
    lane = lax.broadcasted_iota(jnp.int32, cos_ref.shape, 1)
    for k in range(ROPE_HEADS):
        cols = slice(k * HEAD_DIM, (k + 1) * HEAD_DIM)
        y = _rms(h_ref[:, cols]) * w_ref[...]
        partner = jnp.where(lane < half, pltpu.roll(y, LANES - half, axis=1), pltpu.roll(y, half, axis=1))
        o_ref[:, cols] = y * cos_ref[...] + partner * sin_ref[...]


def _rope_tables(pos):
    inv = ROPE_THETA ** (-jnp.arange(0, ROT_DIM, 2, dtype=F32) / ROT_DIM)
    ang = pos.astype(F32)[:, None] * inv[None, :]
    cos, sin = jnp.cos(ang), jnp.sin(ang)
    rest = HEAD_DIM - ROT_DIM
    cos_t = jnp.concatenate([cos, cos, jnp.ones((pos.shape[0], rest), F32)], axis=-1)
    sin_t = jnp.concatenate([-sin, sin, jnp.zeros((pos.shape[0], rest), F32)], axis=-1)
    return cos_t, sin_t


def _qk_rope(h3, qk_w, cos_t, sin_t):
    b, t, _ = h3.shape
    tt = min(1024, t)
    width = ROPE_HEADS * HEAD_DIM
    nq = ATT_WIDTH // width
    off = 4 * A_WIDTH // width
    return pl.pallas_call(
        _qk_rope_kernel,
        grid=(b, t // tt, 2 * nq),
        in_specs=[pl.BlockSpec((None, tt, width), lambda bi, ti, hi: (bi, ti, off + hi)),
                  pl.BlockSpec((None, 1, HEAD_DIM), lambda bi, ti, hi: (hi // nq, 0, 0)),
                  pl.BlockSpec((tt, HEAD_DIM), lambda bi, ti, hi: (ti, 0)),
                  pl.BlockSpec((tt, HEAD_DIM), lambda bi, ti, hi: (ti, 0))],
        out_specs=pl.BlockSpec((None, tt, width), lambda bi, ti, hi: (bi, ti, hi)),
        out_shape=jax.ShapeDtypeStruct((b, t, 2 * ATT_WIDTH), F32),
        compiler_params=_cparams("parallel", "parallel", "arbitrary"),
        name="qk_rope",
    )(h3, qk_w, cos_t, sin_t)


def _softmax_block(s_list, v_list):
    m = s_list[0].max(axis=-1, keepdims=True)
    for s in s_list[1:]:
        m = jnp.maximum(m, s.max(axis=-1, keepdims=True))
    den = None
    o = None
    for s, v in zip(s_list, v_list):
        p = jnp.exp(s - m)
        d = jnp.sum(p, axis=-1, keepdims=True)
        pv = jnp.dot(p.astype(BF16), v.astype(BF16), preferred_element_type=F32)
        den = d if den is None else den + d
        o = pv if o is None else o + pv
    return o / den, m + jnp.log(den)


def _attn_kernel(*refs, tq):
    ins = refs[:15]
    o_ref = refs[15]
    og_refs, lse_refs = refs[16:16 + N_GROUPS], refs[16 + N_GROUPS:16 + 2 * N_GROUPS]
    ti = pl.program_id(2)
    scale = HEAD_DIM ** -0.5
    qi = lax.broadcasted_iota(jnp.int32, (ATT_BLOCK, ATT_BLOCK), 0)
    ki = lax.broadcasted_iota(jnp.int32, (ATT_BLOCK, ATT_BLOCK), 1)
    for g, (win, dil) in enumerate(ATT_GROUPS):
        q_ref, kc_ref, kp_ref, vc_ref, vp_ref = ins[5 * g:5 * g + 5]
        span = ATT_BLOCK * dil
        nblk = tq // span

        def load(ref, start, dil=dil):
            if dil == 1:
                return ref[pl.ds(start, ATT_BLOCK), :]
            return ref[pl.ds(start, ATT_BLOCK, stride=dil), :]

        def body(i, carry, dil=dil, span=span, nblk=nblk, og_ref=og_refs[g], lse_ref=lse_refs[g],
                 q_ref=q_ref, kc_ref=kc_ref, kp_ref=kp_ref, vc_ref=vc_ref, vp_ref=vp_ref, load=load):
            r = i // nblk
            n = i % nblk
            start = r + span * n
            qb = load(q_ref, start).astype(BF16)
            kc = load(kc_ref, start)
            vc = load(vc_ref, start)
            last = r + span * (nblk - 1)
            if nblk == 1:
                kp = load(kp_ref, last)
                vp = load(vp_ref, last)
                has_prev = ti > 0
            else:
                inner = jnp.maximum(start - span, r)
                first = n == 0
                kp = jnp.where(first, load(kp_ref, last), load(kc_ref, inner))
                vp = jnp.where(first, load(vp_ref, last), load(vc_ref, inner))
                has_prev = jnp.logical_or(ti > 0, n > 0)
            s_c = lax.dot_general(qb, kc.astype(BF16), NT_DIMS, preferred_element_type=F32) * scale
            s_p = lax.dot_general(qb, kp.astype(BF16), NT_DIMS, preferred_element_type=F32) * scale
            s_c = jnp.where(ki <= qi, s_c, -jnp.inf)
            s_p = jnp.where(jnp.logical_and(ki >= qi, has_prev), s_p, -jnp.inf)
            o, lse = _softmax_block([s_c, s_p], [vc, vp])
            if dil == 1:
                og_ref[pl.ds(start, ATT_BLOCK), :] = o
                lse_ref[pl.ds(start, ATT_BLOCK), :] = jnp.broadcast_to(lse, o.shape)
            else:
                og_ref[pl.ds(start, ATT_BLOCK, stride=dil), :] = o
                lse_ref[pl.ds(start, ATT_BLOCK, stride=dil), :] = jnp.broadcast_to(lse, o.shape)
            return carry

        lax.fori_loop(0, tq // ATT_BLOCK, body, 0)

    l0, l1, l2 = lse_refs[0][...], lse_refs[1][...], lse_refs[2][...]
    m = jnp.maximum(jnp.maximum(l0, l1), l2)
    e0, e1, e2 = jnp.exp(l0 - m), jnp.exp(l1 - m), jnp.exp(l2 - m)
    z = e0 + e1 + e2
    o_ref[...] = (e0 / z) * og_refs[0][...] + (e1 / z) * og_refs[1][...] + (e2 / z) * og_refs[2][...]


def _attn_prompt(qk3, h3):
    b, t, _ = qk3.shape
    tq = ATT_TILE
    assert t % tq == 0
    nh = N_GROUPS * G_HEADS
    voff = (4 * A_WIDTH + 2 * ATT_WIDTH) // HEAD_DIM
    cur = lambda off: pl.BlockSpec((None, tq, HEAD_DIM), lambda bi, ji, ti, off=off: (bi, ti, off + ji))
    prev = lambda off: pl.BlockSpec((None, tq, HEAD_DIM),
                                    lambda bi, ji, ti, off=off: (bi, jnp.maximum(ti - 1, 0), off + ji))
    in_specs, args = [], []
    for g in range(N_GROUPS):
        in_specs += [cur(g * G_HEADS), cur(nh + g * G_HEADS), prev(nh + g * G_HEADS),
                     cur(voff + g * G_HEADS), prev(voff + g * G_HEADS)]
        args += [qk3, qk3, qk3, h3, h3]
    return pl.pallas_call(
        functools.partial(_attn_kernel, tq=tq),
        grid=(b, G_HEADS, t // tq),
        in_specs=in_specs,
        out_specs=pl.BlockSpec((None, tq, HEAD_DIM), lambda bi, ji, ti: (bi, ti, ji)),
        out_shape=jax.ShapeDtypeStruct((b, t, G_HEADS * HEAD_DIM), F32),
        scratch_shapes=[pltpu.VMEM((tq, HEAD_DIM), F32)] * (2 * N_GROUPS),
        compiler_params=_cparams("parallel", "parallel", "arbitrary"),
        name="attn_prompt",
    )(*args)


def _attn_sample_kernel(*refs, tp, t_valid):
    o_ref = refs[12]
    scale = HEAD_DIM ** -0.5
    outs, lses = [], []
    for g, (win, dil) in enumerate(ATT_GROUPS):
        c_ref, q_ref, k_ref, v_ref = refs[4 * g:4 * g + 4]
        n_back = win // dil
        w = c_ref.shape[1]
        qb = q_ref[...].astype(BF16)
        s_b = lax.dot_general(qb, c_ref[0].astype(BF16), NT_DIMS, preferred_element_type=F32) * scale
        s_n = lax.dot_general(qb, k_ref[...].astype(BF16), NT_DIMS, preferred_element_type=F32) * scale
        tq_b = lax.broadcasted_iota(jnp.int32, (tp, w), 0)
        rb = lax.broadcasted_iota(jnp.int32, (tp, w), 1)
        delta = w + tq_b - rb
        ok_b = jnp.logical_and(delta % dil == 0, delta // dil <= n_back)
        tq_n = lax.broadcasted_iota(jnp.int32, (tp, tp), 0)
        tn = lax.broadcasted_iota(jnp.int32, (tp, tp), 1)
        dn = tq_n - tn
        ok_n = (dn >= 0) & (dn % dil == 0) & (dn // dil <= n_back) & (tn < t_valid)
        s_b = jnp.where(ok_b, s_b, -jnp.inf)
        s_n = jnp.where(ok_n, s_n, -jnp.inf)
        o, lse = _softmax_block([s_b, s_n], [c_ref[1], v_ref[...]])
        outs.append(o)
        lses.append(lse)
    m = jnp.maximum(jnp.maximum(lses[0], lses[1]), lses[2])
    es = [jnp.exp(l - m) for l in lses]
    z = es[0] + es[1] + es[2]
    o_ref[...] = (es[0] / z) * outs[0] + (es[1] / z) * outs[1] + (es[2] / z) * outs[2]


def _attn_sample(qk3, h3, caches, t_valid):
    b, tp, _ = qk3.shape
    nh = N_GROUPS * G_HEADS
    voff = (4 * A_WIDTH + 2 * ATT_WIDTH) // HEAD_DIM
    row = lambda off: pl.BlockSpec((None, tp, HEAD_DIM), lambda bi, ji, off=off: (bi, 0, off + ji))
    in_specs, args = [], []
    for g in range(N_GROUPS):
        w = caches[g].shape[3]
        in_specs += [pl.BlockSpec((None, 2, None, w, HEAD_DIM), lambda bi, ji: (bi, 0, ji, 0, 0)),
                     row(g * G_HEADS), row(nh + g * G_HEADS), row(voff + g * G_HEADS)]
        args += [caches[g], qk3, qk3, h3]
    return pl.pallas_call(
        functools.partial(_attn_sample_kernel, tp=tp, t_valid=t_valid),
        grid=(b, G_HEADS),
        in_specs=in_specs,
        out_specs=pl.BlockSpec((None, tp, HEAD_DIM), lambda bi, ji: (bi, 0, ji)),
        out_shape=jax.ShapeDtypeStruct((b, tp, G_HEADS * HEAD_DIM), F32),
        compiler_params=_cparams("parallel", "arbitrary"),
        name="attn_sample",
    )(*args)


def _out_proj_kernel(oa_ref, ob_ref, ga0_ref, ga1_ref, gb0_ref, gb1_ref, x_ref, g1_ref, sc_ref, sh_ref,
                     nw_ref, wa_ref, wb_ref, wo_ref, x1_ref, n2_ref):
    ya = jnp.dot(oa_ref[...].astype(BF16), wa_ref[...], preferred_element_type=F32)
    yb = jnp.dot(ob_ref[...].astype(BF16), wb_ref[...], preferred_element_type=F32)
    gate_a = _sigmoid(jnp.concatenate([ga0_ref[...], ga1_ref[...]], axis=-1))
    gate_b = _sigmoid(jnp.concatenate([gb0_ref[...], gb1_ref[...]], axis=-1))
    mix = jnp.dot((gate_a * ya + gate_b * yb).astype(BF16), wo_ref[...], preferred_element_type=F32)
    x1 = x_ref[...] + g1_ref[...] * mix
    x1_ref[...] = x1
    n2_ref[...] = (_rms(x1) * nw_ref[...]) * (1.0 + sc_ref[...]) + sh_ref[...]


def _out_proj(oa2, ob2, h2, x2, g1, sc2, sh2, nw2, wa_bf, wb_bf, wo_bf, rows_per_group):
    n, d = x2.shape
    tm = min(512, n, rows_per_group if g1.ndim == 3 else n)
    gw = 512
    goff = (4 * A_WIDTH + 3 * ATT_WIDTH) // gw
    rows = lambda width: pl.BlockSpec((tm, width), lambda i: (i, 0))
    gcol = lambda k: pl.BlockSpec((tm, gw), lambda i, k=k: (i, goff + k))
    full = lambda a: pl.BlockSpec(a.shape, lambda i: (0, 0))
    mod = lambda a: _mod_spec(a, rows_per_group, tm, 1)
    return pl.pallas_call(
        _out_proj_kernel,
        grid=(n // tm,),
        in_specs=[rows(A_WIDTH), rows(G_HEADS * HEAD_DIM), gcol(0), gcol(1), gcol(2), gcol(3), rows(d),
                  mod(g1), mod(sc2), mod(sh2), pl.BlockSpec((1, d), lambda i: (0, 0)),
                  full(wa_bf), full(wb_bf), full(wo_bf)],
        out_specs=[rows(d), rows(d)],
        out_shape=[jax.ShapeDtypeStruct((n, d), F32), jax.ShapeDtypeStruct((n, d), F32)],
        compiler_params=_cparams("parallel"),
        name="out_proj",
    )(oa2, ob2, h2, h2, h2, h2, x2, g1, sc2, sh2, nw2.reshape(1, d), wa_bf, wb_bf, wo_bf)


def _topk_rows(s, k, extra=None):
    rows = s.shape[0]
    ridx = lax.broadcasted_iota(jnp.int32, s.shape, 0).astype(F32)
    vals, idxs, extras = [], [], []
    for _ in range(k):
        m = jnp.max(s, axis=0, keepdims=True)
        idx = jnp.min(jnp.where(s == m, ridx, float(rows)), axis=0, keepdims=True)
        hit = ridx == idx
        vals.append(m)
        idxs.append(idx.astype(jnp.int32))
        if extra is not None:
            extras.append(jnp.sum(jnp.where(hit, extra, 0), axis=0, keepdims=True))
        s = jnp.where(hit, -jnp.inf, s)
    return vals, idxs, extras


CAND_CELLS = tuple((a, b) for a in range(P_TOPK) for b in range(P_TOPK) if (a + 1) * (b + 1) <= P_TOPK)
CAND_PAD = -len(CAND_CELLS) % SUBLANES
SEL_TILE = 128


def _peer_sel_kernel(n2_ref, wq_ref, qn_ref, k1_ref, k2_ref, e_ref, g_ref, q_ref, *, tm):
    q = jnp.dot(n2_ref[...].astype(BF16), wq_ref[...], preferred_element_type=F32)
    q_ref[...] = _rms(q) * qn_ref[...]
    hi = lax.Precision.HIGHEST

    def tile(ti, carry):
        q = q_ref[pl.ds(pl.multiple_of(ti * SEL_TILE, SEL_TILE), SEL_TILE), :]
        s1 = lax.dot_general(k1_ref[...], q[:, :P_HALF], NT_DIMS, preferred_element_type=F32, precision=hi)
        s2 = lax.dot_general(k2_ref[...], q[:, P_HALF:], NT_DIMS, preferred_element_type=F32, precision=hi)
        v1, i1, _ = _topk_rows(s1, P_TOPK)
        v2, i2, _ = _topk_rows(s2, P_TOPK)
        cand = [v1[a] + v2[b] for a, b in CAND_CELLS]
        cidx = [i1[a] * N_KEYS + i2[b] for a, b in CAND_CELLS]
        cand += [jnp.full_like(cand[0], -jnp.inf)] * CAND_PAD
        cidx += [jnp.zeros_like(cidx[0])] * CAND_PAD
        sc, _, eidx = _topk_rows(jnp.concatenate(cand, axis=0), P_TOPK, extra=jnp.concatenate(cidx, axis=0))
        sc = jnp.concatenate(sc, axis=0)
        p = jnp.exp(sc - jnp.max(sc, axis=0, keepdims=True))
        cols = pl.ds(pl.multiple_of(ti * SEL_TILE, SEL_TILE), SEL_TILE)
        g_ref[:, cols] = p / jnp.sum(p, axis=0, keepdims=True)
        e_ref[:, cols] = jnp.concatenate(eidx, axis=0)
        return carry

    lax.fori_loop(0, tm // SEL_TILE, tile, 0)


def _peer_sel(n2, wq_bf, qn_w, k1, k2):
    n, d = n2.shape
    tm = min(256, n)
    assert tm % SEL_TILE == 0
    return pl.pallas_call(
        functools.partial(_peer_sel_kernel, tm=tm),
        grid=(n // tm, P_HEADS),
        in_specs=[pl.BlockSpec((tm, d), lambda i, h: (i, 0)),
                  pl.BlockSpec((d, P_QDIM), lambda i, h: (0, h)),
                  pl.BlockSpec((1, P_QDIM), lambda i, h: (0, 0)),
                  pl.BlockSpec((None, N_KEYS, P_HALF), lambda i, h: (h, 0, 0)),
                  pl.BlockSpec((None, N_KEYS, P_HALF), lambda i, h: (h, 0, 0))],
        out_specs=[pl.BlockSpec((None, P_TOPK, tm), lambda i, h: (h, 0, i)),
                   pl.BlockSpec((None, P_TOPK, tm), lambda i, h: (h, 0, i))],
        out_shape=[jax.ShapeDtypeStruct((P_HEADS, P_TOPK, n), jnp.int32),
                   jax.ShapeDtypeStruct((P_HEADS, P_TOPK, n), F32)],
        scratch_shapes=[pltpu.VMEM((tm, P_QDIM), F32)],
        compiler_params=_cparams("parallel", "arbitrary"),
        name="peer_sel",
    )(n2, wq_bf, qn_w.reshape(1, P_QDIM), k1, k2)


HI_HALF = 0xFFFF0000
ROWS_PER_EXPERT = 4


HALF_SLOTS = P_PAIRS // 2


def _pack_table(tab):
    e, d = tab.shape
    assert d == 2 * ROWS_PER_EXPERT * LANES
    bits = lax.bitcast_convert_type(tab.astype(BF16), jnp.uint16).astype(jnp.uint32)
    bits = bits.reshape(e, 2, ROWS_PER_EXPERT, LANES)
    packed = (bits[:, 0] | (bits[:, 1] << 16)).reshape(e * ROWS_PER_EXPERT, LANES)
    return jnp.pad(packed, ((ROWS_PER_EXPERT, ROWS_PER_EXPERT), (0, 0)))


def _table_rows(e_rm):
    first = (e_rm + 1) * ROWS_PER_EXPERT
    slot = lax.broadcasted_iota(jnp.int32, e_rm.shape, 1)
    return jnp.where(slot < HALF_SLOTS, first, first - ROWS_PER_EXPERT)


def _pair_halves(tab_ref, row_a, row_b, top):
    tile = jnp.where(top, tab_ref[pl.ds(row_a, SUBLANES), :], tab_ref[pl.ds(row_b, SUBLANES), :])
    lo = lax.bitcast_convert_type(tile << jnp.uint32(16), F32)
    hi = lax.bitcast_convert_type(tile & jnp.uint32(HI_HALF), F32)
    return lo, hi


def _top_half():
    return lax.broadcasted_iota(jnp.int32, (SUBLANES, LANES), 0) < ROWS_PER_EXPERT


def _gelu(x):
    return 0.5 * x * (1.0 + jnp.tanh(0.7978845608028654 * (x + 0.044715 * x * x * x)))


def _smem_block_copies(srcs, dsts, sem, tb):
    rows = pl.ds(pl.program_id(0) * (tb * P_PAIRS), tb * P_PAIRS)
    return [pltpu.make_async_copy(src.at[rows], dst, sem.at[k]) for k, (src, dst) in enumerate(zip(srcs, dsts))]


TILE_CHUNK = 16
N_CHUNKS = HALF_SLOTS // TILE_CHUNK
U_GROUP = 8


def _stage_sublane(sum_ref, k):
    return sum_ref[pl.ds(k, HALF_SLOTS, stride=SUBLANES), :]


def _peer_u_row(tab_ref, idx_smem, x_ref, t, fill_ref, sum_ref):
    x_t = x_ref[t]
    x_lo = jnp.concatenate([x_t[:ROWS_PER_EXPERT], x_t[:ROWS_PER_EXPERT]], axis=0)
    x_hi = jnp.concatenate([x_t[ROWS_PER_EXPERT:], x_t[ROWS_PER_EXPERT:]], axis=0)
    top = _top_half()

    def chunk(c, parts):
        q0 = c * TILE_CHUNK
        base = t * P_PAIRS + q0
        for p in range(TILE_CHUNK):
            lo, hi = _pair_halves(tab_ref, idx_smem[base + p], idx_smem[base + HALF_SLOTS + p], top)
            fill_ref[pl.ds(pl.multiple_of((q0 + p) * SUBLANES, SUBLANES), SUBLANES), :] = lo * x_lo + hi * x_hi
        return (parts[0] + _stage_sublane(sum_ref, c), parts[1] + _stage_sublane(sum_ref, ROWS_PER_EXPERT + c))

    assert N_CHUNKS == ROWS_PER_EXPERT
    zero = jnp.zeros((HALF_SLOTS, LANES), F32)
    return jnp.concatenate(lax.fori_loop(0, N_CHUNKS, chunk, (zero, zero)), axis=0)


def _peer_u_kernel(idx_hbm, x_ref, g_ref, tab_ref, w_ref, idx_smem, sem, stage_a, stage_b, part_ref, act_ref, *, tb):
    copies = _smem_block_copies((idx_hbm,), (idx_smem,), sem, tb)
    for cp in copies:
        cp.start()
    for cp in copies:
        cp.wait()
    stage_b[...] = jnp.zeros(stage_b.shape, F32)

    def two_rows(j, carry):
        t0 = 2 * j
        part_ref[jnp.maximum(t0 - 1, 0)] = _peer_u_row(tab_ref, idx_smem, x_ref, t0, stage_a, stage_b)
        part_ref[t0] = _peer_u_row(tab_ref, idx_smem, x_ref, t0 + 1, stage_b, stage_a)
        return carry

    lax.fori_loop(0, tb // 2, two_rows, 0)
    halves = []
    for h in range(2):
        s = _stage_sublane(stage_b, h * ROWS_PER_EXPERT)
        for k in range(1, ROWS_PER_EXPERT):
            s = s + _stage_sublane(stage_b, h * ROWS_PER_EXPERT + k)
        halves.append(s)
    part_ref[tb - 1] = jnp.concatenate(halves, axis=0)

    eye = (lax.broadcasted_iota(jnp.int32, (P_PAIRS, LANES), 0)
           == lax.broadcasted_iota(jnp.int32, (P_PAIRS, LANES), 1))

    def lane_sums(gi, carry):
        rows = []
        for k in range(U_GROUP):
            col = jnp.sum(part_ref[gi * U_GROUP + k], axis=1, keepdims=True)
            rows.append(jnp.sum(jnp.where(eye, col, 0.0), axis=0, keepdims=True))
        act_ref[pl.ds(pl.multiple_of(gi * U_GROUP, U_GROUP), U_GROUP), :] = jnp.concatenate(rows, axis=0)
        return carry

    lax.fori_loop(0, tb // U_GROUP, lane_sums, 0)
    w_ref[...] = g_ref[...] * _gelu(act_ref[...])


def _peer_u(idx, x3, g2d, tab):
    n = x3.shape[0]
    tb = min(128, n)
    assert tb % U_GROUP == 0 and n % tb == 0
    return pl.pallas_call(
        functools.partial(_peer_u_kernel, tb=tb),
        grid=(n // tb,),
        in_specs=[pl.BlockSpec(memory_space=pl.ANY),
                  pl.BlockSpec((tb, SUBLANES, LANES), lambda i: (i, 0, 0)),
                  pl.BlockSpec((tb, P_PAIRS), lambda i: (i, 0)),
                  pl.BlockSpec(tab.shape, lambda i: (0, 0), pipeline_mode=pl.Buffered(1))],
        out_specs=pl.BlockSpec((tb, P_PAIRS), lambda i: (i, 0)),
        out_shape=jax.ShapeDtypeStruct((n, P_PAIRS), F32),
        scratch_shapes=[pltpu.SMEM((tb * P_PAIRS,), jnp.int32), pltpu.SemaphoreType.DMA((1,)),
                        pltpu.VMEM((HALF_SLOTS * SUBLANES, LANES), F32),
                        pltpu.VMEM((HALF_SLOTS * SUBLANES, LANES), F32),
                        pltpu.VMEM((tb, P_PAIRS, LANES), F32),
                        pltpu.VMEM((tb, P_PAIRS), F32)],
        compiler_params=_cparams("arbitrary"),
        name="peer_u",
    )(idx, x3, g2d, tab)


def _peer_v_kernel(idx_hbm, w_hbm, x1_ref, g2_ref, tab_ref, y_ref, idx_smem, w_smem, sem, out_ref, *, tb):
    copies = _smem_block_copies((idx_hbm, w_hbm), (idx_smem, w_smem), sem, tb)
    for cp in copies:
        cp.start()
    for cp in copies:
        cp.wait()
    nacc = 2
    top = _top_half()

    def tok(t, carry):
        def chunk(c, accs):
            los, his = list(accs[:nacc]), list(accs[nacc:])
            base = t * P_PAIRS + c * TILE_CHUNK
            for p in range(TILE_CHUNK):
                a, b = base + p, base + HALF_SLOTS + p
                lo, hi = _pair_halves(tab_ref, idx_smem[a], idx_smem[b], top)
                w = jnp.where(top, w_smem[a], w_smem[b])
                los[p % nacc] = los[p % nacc] + w * lo
                his[p % nacc] = his[p % nacc] + w * hi
            return tuple(los + his)

        zero = jnp.zeros((SUBLANES, LANES), F32)
        accs = lax.fori_loop(0, N_CHUNKS, chunk, (zero,) * (2 * nacc))
        lo, hi = accs[0] + accs[1], accs[2] + accs[3]
        out_ref[t] = jnp.concatenate([lo[:ROWS_PER_EXPERT] + lo[ROWS_PER_EXPERT:],
                                      hi[:ROWS_PER_EXPERT] + hi[ROWS_PER_EXPERT:]], axis=0)
        return carry

    lax.fori_loop(0, tb, tok, 0)
    y_ref[...] = x1_ref[...] + g2_ref[...] * out_ref[...]


def _peer_v(idx, w2d, x13, g2, tab, rows_per_group):
    n = x13.shape[0]
    tb = min(256, n, rows_per_group if g2.ndim == 4 else n)
    if g2.ndim == 4:
        g2_spec = pl.BlockSpec((None, 1, SUBLANES, LANES), lambda i: (i * tb // rows_per_group, 0, 0, 0))
    else:
        g2_spec = pl.BlockSpec((tb, SUBLANES, LANES), lambda i: (i, 0, 0))
    return pl.pallas_call(
        functools.partial(_peer_v_kernel, tb=tb),
        grid=(n // tb,),
        in_specs=[pl.BlockSpec(memory_space=pl.ANY),
                  pl.BlockSpec(memory_space=pl.ANY),
                  pl.BlockSpec((tb, SUBLANES, LANES), lambda i: (i, 0, 0)),
                  g2_spec,
                  pl.BlockSpec(tab.shape, lambda i: (0, 0), pipeline_mode=pl.Buffered(1))],
        out_specs=pl.BlockSpec((tb, SUBLANES, LANES), lambda i: (i, 0, 0)),
        out_shape=jax.ShapeDtypeStruct((n, SUBLANES, LANES), F32),
        scratch_shapes=[pltpu.SMEM((tb * P_PAIRS,), jnp.int32), pltpu.SMEM((tb * P_PAIRS,), F32),
                        pltpu.SemaphoreType.DMA((2,)), pltpu.VMEM((tb, SUBLANES, LANES), F32)],
        compiler_params=_cparams("arbitrary"),
        name="peer_v",
    )(idx, w2d, x13, g2, tab)


def _group_mod(m):
    return m[:, None, :]


def _layer(x, mod, pos, lb, s0, kv_bufs, wts):
    (norm1_w, norm2_w, w_in_bf, hgrn_norm_w, qk_w, wa_bf, wb_bf, wo_bf, wq_bf, peer_qn_w,
     peer_k1, peer_k2, u_tab, v_tab) = wts
    b, t, d = x.shape
    n = b * t
    sample = kv_bufs is not None
    sh1, sc1, g1, sh2, sc2, g2 = jnp.split(mod, 6, axis=-1)
    if sample:
        per_row = lambda m: jnp.repeat(m, t, axis=0)
        sh1, sc1, g1, sh2, sc2 = map(per_row, (sh1, sc1, g1, sh2, sc2))
        g2 = per_row(g2).reshape(n, SUBLANES, LANES)
    else:
        sh1, sc1, g1, sh2, sc2 = map(_group_mod, (sh1, sc1, g1, sh2, sc2))
        g2 = g2.reshape(b, 1, SUBLANES, LANES)
    x2 = x.reshape(n, d)
    h2 = _in_proj(x2, sc1, sh1, norm1_w, w_in_bf, t)
    h3 = h2.reshape(b, t, IN_COLS)

    if sample:
        tp = HGRN_SUB
        h3p = jnp.pad(h3, ((0, 0), (0, tp - t), (0, 0)))
        pos_p = pos[0] + jnp.arange(tp)
        oa, s_new = _hgrn(h3p, lb, s0, hgrn_norm_w, t_valid=t)
        cos_t, sin_t = _rope_tables(pos_p)
        qk3 = _qk_rope(h3p, qk_w, cos_t, sin_t)
        ob = _attn_sample(qk3, h3p, kv_bufs, t)
        oa, ob, qk3 = oa[:, :t], ob[:, :t], qk3[:, :t]
    else:
        oa, s_new = _hgrn(h3, lb, s0, hgrn_norm_w)
        cos_t, sin_t = _rope_tables(pos)
        qk3 = _qk_rope(h3, qk_w, cos_t, sin_t)
        ob = _attn_prompt(qk3, h3)

    kn = qk3[..., ATT_WIDTH:].reshape(b, t, N_GROUPS * G_HEADS, HEAD_DIM)
    voff = 4 * A_WIDTH + 2 * ATT_WIDTH
    vn = h3[..., voff:voff + ATT_WIDTH].reshape(b, t, N_GROUPS * G_HEADS, HEAD_DIM)
    new_kv = []
    for gi, (win, _) in enumerate(ATT_GROUPS):
        keep = t if sample else min(win, t)
        hs = slice(gi * G_HEADS, (gi + 1) * G_HEADS)
        kg = kn[:, t - keep:, hs].transpose(0, 2, 1, 3)
        vg = vn[:, t - keep:, hs].transpose(0, 2, 1, 3)
        new_kv.append(jnp.stack([kg, vg], axis=1))

    x1, n2 = _out_proj(oa.reshape(n, A_WIDTH), ob.reshape(n, G_HEADS * HEAD_DIM), h2, x2, g1, sc2, sh2,
                       norm2_w, wa_bf, wb_bf, wo_bf, t)
    e_t, g_t = _peer_sel(n2, wq_bf, peer_qn_w, peer_k1, peer_k2)
    idx = _table_rows(e_t.reshape(P_PAIRS, n).T).reshape(n * P_PAIRS)
    g2d = g_t.reshape(P_PAIRS, n).T
    w = _peer_u(idx, n2.reshape(n, SUBLANES, LANES), g2d, u_tab)
    y3 = _peer_v(idx, w.reshape(n * P_PAIRS), x1.reshape(n, SUBLANES, LANES), g2, v_tab, t)
    return y3.reshape(b, t, d), new_kv, s_new


def kernel(x_prompt, x_sample, cache_kv_g1, cache_kv_g2, cache_kv_g3, state_hgrn, c_prompt, c_sample, w_ada, b_ada, norm1_w, norm2_w, w_in, lb_logits, hgrn_norm_w, q_norm_w, k_norm_w, w_branch_a, w_branch_b, w_out, peer_wq, peer_qn_w, peer_k1, peer_k2, peer_u, peer_v):
    depth = w_ada.shape[0]
    bp, tp_len, _ = x_prompt.shape
    bs, ts_len, _ = x_sample.shape
    pos_p = jnp.arange(tp_len)
    pos_s = PAST_LEN + jnp.arange(ts_len)
    lb_all = jnp.cumsum(jax.nn.softmax(lb_logits.astype(F32), axis=0), axis=0)
    caches = (cache_kv_g1, cache_kv_g2, cache_kv_g3)
    c_all = jnp.concatenate([c_prompt, c_sample], axis=0)
    yp, ys = x_prompt, x_sample
    kvp, kvs = ([], [], []), ([], [], [])
    sp_list, ss_list = [], []
    for l in range(depth):
        mod = _ada(c_all, w_ada[l], b_ada[l])
        wts = (norm1_w[l], norm2_w[l], w_in[l].astype(BF16), hgrn_norm_w[l],
               jnp.stack([q_norm_w[l], k_norm_w[l]])[:, None, :],
               w_branch_a[l].astype(BF16), w_branch_b[l].astype(BF16), w_out[l].astype(BF16),
               peer_wq[l].astype(BF16), peer_qn_w[l], peer_k1[l], peer_k2[l],
               _pack_table(peer_u[l]), _pack_table(peer_v[l]))
        lb = lb_all[l].reshape(A_HEADS, A_DK)
        s0p = jnp.zeros((bp, A_HEADS, A_DK, A_DK), F32)
        yp, nkv_p, sp = _layer(yp, mod[:bp], pos_p, lb, s0p, None, wts)
        ys, nkv_s, ss = _layer(ys, mod[bp:], pos_s, lb, state_hgrn[l],
                               tuple(c[l] for c in caches), wts)
        for gi in range(N_GROUPS):
            kvp[gi].append(nkv_p[gi])
            kvs[gi].append(nkv_s[gi])
        sp_list.append(sp)
        ss_list.append(ss)
    return (yp, ys, jnp.stack(kvp[0]), jnp.stack(kvp[1]), jnp.stack(kvp[2]), jnp.stack(sp_list),
            jnp.stack(kvs[0]), jnp.stack(kvs[1]), jnp.stack(kvs[2]), jnp.stack(ss_list))
```

```python
import functools

import jax
import jax.numpy as jnp
from jax import lax
from jax.experimental import pallas as pl
from jax.experimental.pallas import tpu as pltpu

F32 = jnp.float32
BF16 = jnp.bfloat16

D_MODEL = 1024
PAST_LEN = 16384
A_HEADS = 4
A_DK = 128
A_WIDTH = A_HEADS * A_DK
ATT_GROUPS = ((128, 1), (512, 4), (2048, 16))
N_GROUPS = 3
G_HEADS = 4
HEAD_DIM = 128
ATT_WIDTH = N_GROUPS * G_HEADS * HEAD_DIM
ATT_BLOCK = 128
ROT_DIM = HEAD_DIM // 4
ROPE_THETA = 500000.0
IN_COLS = 4 * A_WIDTH + 3 * ATT_WIDTH + 2 * D_MODEL
P_HEADS = 8
N_KEYS = 128
P_QDIM = 256
P_HALF = P_QDIM // 2
P_TOPK = 16
P_PAIRS = P_HEADS * P_TOPK
EPS = 1e-6

LANES = 128
SUBLANES = 8
HGRN_SUB = 16
ATT_TILE = ATT_BLOCK * ATT_GROUPS[-1][1]
ATT_UNROLL = 4
VMEM_LIMIT = 56 * 1024 * 1024

NT_DIMS = (((1,), (1,)), ((), ()))
TN_DIMS = (((0,), (0,)), ((), ()))


def _cparams(*sem):
    return pltpu.CompilerParams(dimension_semantics=sem, vmem_limit_bytes=VMEM_LIMIT)


def _rms(x):
    return x * lax.rsqrt(jnp.mean(x * x, axis=-1, keepdims=True) + EPS)


def _sigmoid(x):
    return 1.0 / (1.0 + jnp.exp(-x))


def _mod_spec(arr, rows_per_group, tm, ngrid):
    if arr.ndim == 3:
        if ngrid == 2:
            return pl.BlockSpec((None, 1, arr.shape[-1]), lambda i, j: (i * tm // rows_per_group, 0, 0))
        return pl.BlockSpec((None, 1, arr.shape[-1]), lambda i: (i * tm // rows_per_group, 0, 0))
    if ngrid == 2:
        return pl.BlockSpec((tm, arr.shape[-1]), lambda i, j: (i, 0))
    return pl.BlockSpec((tm, arr.shape[-1]), lambda i: (i, 0))


def _ada_kernel(c_ref, w_ref, b_ref, o_ref):
    c = c_ref[...]
    s = c * _sigmoid(c)
    o_ref[...] = jnp.dot(s, w_ref[...], preferred_element_type=F32) + b_ref[...]


def _ada(c, w, b):
    rows, d = c.shape
    n = -(-rows // SUBLANES) * SUBLANES
    c = jnp.pad(c, ((0, n - rows), (0, 0)))
    cols = w.shape[1]
    tn = 1024
    out = pl.pallas_call(
        _ada_kernel,
        grid=(cols // tn,),
        in_specs=[pl.BlockSpec((n, d), lambda j: (0, 0)),
                  pl.BlockSpec((d, tn), lambda j: (0, j)),
                  pl.BlockSpec((1, tn), lambda j: (0, j))],
        out_specs=pl.BlockSpec((n, tn), lambda j: (0, j)),
        out_shape=jax.ShapeDtypeStruct((n, cols), F32),
        compiler_params=_cparams("arbitrary"),
        name="ada",
    )(c, w, b.reshape(1, cols))
    return out[:rows]


def _in_proj_kernel(x_ref, sc_ref, sh_ref, nw_ref, w_ref, o_ref, xn_ref):
    @pl.when(pl.program_id(1) == 0)
    def _():
        y = _rms(x_ref[...]) * nw_ref[...]
        y = y * (1.0 + sc_ref[...]) + sh_ref[...]
        xn_ref[...] = y.astype(BF16)

    o_ref[...] = jnp.dot(xn_ref[...], w_ref[...], preferred_element_type=F32)


def _in_proj(x2, sc, sh, nw, w_bf, rows_per_group):
    n, d = x2.shape
    cols = w_bf.shape[1]
    tm = min(1024, n, rows_per_group if sc.ndim == 3 else n)
    tn = cols // 4
    assert tn % LANES == 0
    return pl.pallas_call(
        _in_proj_kernel,
        grid=(n // tm, cols // tn),
        in_specs=[pl.BlockSpec((tm, d), lambda i, j: (i, 0)),
                  _mod_spec(sc, rows_per_group, tm, 2),
                  _mod_spec(sh, rows_per_group, tm, 2),
                  pl.BlockSpec((1, d), lambda i, j: (0, 0)),
                  pl.BlockSpec((d, tn), lambda i, j: (0, j))],
        out_specs=pl.BlockSpec((tm, tn), lambda i, j: (i, j)),
        out_shape=jax.ShapeDtypeStruct((n, cols), F32),
        scratch_shapes=[pltpu.VMEM((tm, d), BF16)],
        compiler_params=_cparams("parallel", "arbitrary"),
        name="in_proj",
    )(x2, sc, sh, nw.reshape(1, d), w_bf)


def _hgrn_kernel(q_ref, f_ref, i_ref, g_ref, lb_ref, s0_ref, nw_ref, o_ref, sout_ref,
                 st_ref, b_ref, k_ref, *, tt, sub, t_valid):
    t = pl.program_id(1)

    @pl.when(t == 0)
    def _():
        for h in range(A_HEADS):
            st_ref[h] = s0_ref[h].T

    lb = lb_ref[...]
    f = lb + (1.0 - lb) * _sigmoid(f_ref[...])
    logf = jnp.log(f)
    kk = 1.0 - f
    if t_valid is not None:
        live = lax.broadcasted_iota(jnp.int32, (tt, 1), 0) < t_valid
        logf = jnp.where(live, logf, 0.0)
        kk = jnp.where(live, kk, 0.0)
    r = lax.broadcasted_iota(jnp.int32, (tt, tt), 0)
    c = lax.broadcasted_iota(jnp.int32, (tt, tt), 1)
    tri = ((r // sub == c // sub) & (c <= r)).astype(F32)
    b_ref[...] = jnp.dot(tri, logf, preferred_element_type=F32, precision=lax.Precision.HIGHEST)
    k_ref[...] = kk

    row = lax.broadcasted_iota(jnp.int32, (sub, 1), 0)
    nw = nw_ref[...]

    def body(i, carry):
        sl = pl.ds(pl.multiple_of(i * sub, sub), sub)
        for h in range(A_HEADS):
            cols = slice(h * A_DK, (h + 1) * A_DK)
            bq = b_ref[sl, cols]
            kq = k_ref[sl, cols]
            qq = q_ref[sl, cols]
            vv = i_ref[sl, cols]
            st = st_ref[h]
            qe = qq * jnp.exp(bq)
            o = lax.dot_general(qe.astype(BF16), st.astype(BF16), NT_DIMS, preferred_element_type=F32)
            for s in range(sub):
                dd = jnp.where(row >= s, bq - bq[s:s + 1, :], -jnp.inf)
                m = qq * kq[s:s + 1, :] * jnp.exp(dd)
                o = o + jnp.sum(m, axis=-1, keepdims=True) * vv[s:s + 1, :]
            bl = bq[sub - 1:sub, :]
            kd = kq * jnp.exp(bl - bq)
            st_ref[h] = st * jnp.exp(bl) + lax.dot_general(vv.astype(BF16), kd.astype(BF16), TN_DIMS,
                                                           preferred_element_type=F32)
            gq = g_ref[sl, cols]
            o_ref[sl, cols] = _rms(o) * nw * (gq * _sigmoid(gq))
        return carry

    lax.fori_loop(0, tt // sub, body, 0)

    @pl.when(t == pl.num_programs(1) - 1)
    def _():
        for h in range(A_HEADS):
            sout_ref[h] = st_ref[h].T


def _hgrn(h3, lb, s0, nw, t_valid=None):
    b, t, _ = h3.shape
    tt = min(256, t)
    sub = min(HGRN_SUB, tt)
    blk = lambda off: pl.BlockSpec((None, tt, A_WIDTH), lambda bi, ti: (bi, ti, off))
    return pl.pallas_call(
        functools.partial(_hgrn_kernel, tt=tt, sub=sub, t_valid=t_valid),
        grid=(b, t // tt),
        in_specs=[blk(0), blk(1), blk(2), blk(3),
                  pl.BlockSpec((1, A_WIDTH), lambda bi, ti: (0, 0)),
                  pl.BlockSpec((None, A_HEADS, A_DK, A_DK), lambda bi, ti: (bi, 0, 0, 0)),
                  pl.BlockSpec((1, A_DK), lambda bi, ti: (0, 0))],
        out_specs=[pl.BlockSpec((None, tt, A_WIDTH), lambda bi, ti: (bi, ti, 0)),
                   pl.BlockSpec((None, A_HEADS, A_DK, A_DK), lambda bi, ti: (bi, 0, 0, 0))],
        out_shape=[jax.ShapeDtypeStruct((b, t, A_WIDTH), F32),
                   jax.ShapeDtypeStruct((b, A_HEADS, A_DK, A_DK), F32)],
        scratch_shapes=[pltpu.VMEM((A_HEADS, A_DK, A_DK), F32), pltpu.VMEM((tt, A_WIDTH), F32),
                        pltpu.VMEM((tt, A_WIDTH), F32)],
        compiler_params=_cparams("parallel", "arbitrary"),
        name="hgrn",
    )(h3, h3, h3, h3, lb.reshape(1, A_WIDTH), s0, nw.reshape(1, A_DK))


ROPE_HEADS = 4


def _qk_rope_kernel(h_ref, w_ref, cos_ref, sin_ref, o_ref):
    half = ROT_DIM // 2
    lane = lax.broadcasted_iota(jnp.int32, cos_ref.shape, 1)
    for k in range(ROPE_HEADS):
        cols = slice(k * HEAD_DIM, (k + 1) * HEAD_DIM)
        y = _rms(h_ref[:, cols]) * w_ref[...]
        partner = jnp.where(lane < half, pltpu.roll(y, LANES - half, axis=1), pltpu.roll(y, half, axis=1))
        o_ref[:, cols] = y * cos_ref[...] + partner * sin_ref[...]


def _rope_tables(pos):
    inv = ROPE_THETA ** (-jnp.arange(0, ROT_DIM, 2, dtype=F32) / ROT_DIM)
    ang = pos.astype(F32)[:, None] * inv[None, :]
    cos, sin = jnp.cos(ang), jnp.sin(ang)
    rest = HEAD_DIM - ROT_DIM
    cos_t = jnp.concatenate([cos, cos, jnp.ones((pos.shape[0], rest), F32)], axis=-1)
    sin_t = jnp.concatenate([-sin, sin, jnp.zeros((pos.shape[0], rest), F32)], axis=-1)
    return cos_t, sin_t


def _qk_rope(h3, qk_w, cos_t, sin_t):
    b, t, _ = h3.shape
    tt = min(1024, t)
    width = ROPE_HEADS * HEAD_DIM
    nq = ATT_WIDTH // width
    off = 4 * A_WIDTH // width
    return pl.pallas_call(
        _qk_rope_kernel,
        grid=(b, t // tt, 2 * nq),
        in_specs=[pl.BlockSpec((None, tt, width), lambda bi, ti, hi: (bi, ti, off + hi)),
                  pl.BlockSpec((None, 1, HEAD_DIM), lambda bi, ti, hi: (hi // nq, 0, 0)),
                  pl.BlockSpec((tt, HEAD_DIM), lambda bi, ti, hi: (ti, 0)),
                  pl.BlockSpec((tt, HEAD_DIM), lambda bi, ti, hi: (ti, 0))],
        out_specs=pl.BlockSpec((None, tt, width), lambda bi, ti, hi: (bi, ti, hi)),
        out_shape=jax.ShapeDtypeStruct((b, t, 2 * ATT_WIDTH), F32),
        compiler_params=_cparams("parallel", "parallel", "arbitrary"),
        name="qk_rope",
    )(h3, qk_w, cos_t, sin_t)


def _softmax_block(s_list, v_list):
    m = s_list[0].max(axis=-1, keepdims=True)
    for s in s_list[1:]:
        m = jnp.maximum(m, s.max(axis=-1, keepdims=True))
    den = None
    o = None
    for s, v in zip(s_list, v_list):
        p = jnp.exp(s - m)
        d = jnp.sum(p, axis=-1, keepdims=True)
        pv = jnp.dot(p.astype(BF16), v.astype(BF16), preferred_element_type=F32)
        den = d if den is None else den + d
        o = pv if o is None else o + pv
    return o / den, m + jnp.log(den)


def _attn_kernel(*refs, tq):
    ins = refs[:15]
    o_ref = refs[15]
    og_refs, lse_refs = refs[16:16 + N_GROUPS], refs[16 + N_GROUPS:16 + 2 * N_GROUPS]
    ti = pl.program_id(2)
    scale = HEAD_DIM ** -0.5
    qi = lax.broadcasted_iota(jnp.int32, (ATT_BLOCK, ATT_BLOCK), 0)
    ki = lax.broadcasted_iota(jnp.int32, (ATT_BLOCK, ATT_BLOCK), 1)
    for g, (win, dil) in enumerate(ATT_GROUPS):
        q_ref, kc_ref, kp_ref, vc_ref, vp_ref = ins[5 * g:5 * g + 5]
        span = ATT_BLOCK * dil
        nblk = tq // span

        def load(ref, start, dil=dil):
            if dil == 1:
                return ref[pl.ds(start, ATT_BLOCK), :]
            return ref[pl.ds(start, ATT_BLOCK, stride=dil), :]

        def body(i, carry, dil=dil, span=span, nblk=nblk, og_ref=og_refs[g], lse_ref=lse_refs[g],
                 q_ref=q_ref, kc_ref=kc_ref, kp_ref=kp_ref, vc_ref=vc_ref, vp_ref=vp_ref, load=load):
            r = i // nblk
            n = i % nblk
            start = r + span * n
            qb = load(q_ref, start).astype(BF16)
            kc = load(kc_ref, start)
            vc = load(vc_ref, start)
            last = r + span * (nblk - 1)
            if nblk == 1:
                kp = load(kp_ref, last)
                vp = load(vp_ref, last)
                has_prev = ti > 0
            else:
                inner = jnp.maximum(start - span, r)
                first = n == 0
                kp = jnp.where(first, load(kp_ref, last), load(kc_ref, inner))
                vp = jnp.where(first, load(vp_ref, last), load(vc_ref, inner))
                has_prev = jnp.logical_or(ti > 0, n > 0)
            s_c = lax.dot_general(qb, kc.astype(BF16), NT_DIMS, preferred_element_type=F32) * scale
            s_p = lax.dot_general(qb, kp.astype(BF16), NT_DIMS, preferred_element_type=F32) * scale
            s_c = jnp.where(ki <= qi, s_c, -jnp.inf)
            s_p = jnp.where(jnp.logical_and(ki >= qi, has_prev), s_p, -jnp.inf)
            o, lse = _softmax_block([s_c, s_p], [vc, vp])
            if dil == 1:
                og_ref[pl.ds(start, ATT_BLOCK), :] = o
                lse_ref[pl.ds(start, ATT_BLOCK), :] = jnp.broadcast_to(lse, o.shape)
            else:
                og_ref[pl.ds(start, ATT_BLOCK, stride=dil), :] = o
                lse_ref[pl.ds(start, ATT_BLOCK, stride=dil), :] = jnp.broadcast_to(lse, o.shape)
            return carry

        lax.fori_loop(0, tq // ATT_BLOCK, body, 0, unroll=ATT_UNROLL)

    l0, l1, l2 = lse_refs[0][...], lse_refs[1][...], lse_refs[2][...]
    m = jnp.maximum(jnp.maximum(l0, l1), l2)
    e0, e1, e2 = jnp.exp(l0 - m), jnp.exp(l1 - m), jnp.exp(l2 - m)
    z = e0 + e1 + e2
    o_ref[...] = (e0 / z) * og_refs[0][...] + (e1 / z) * og_refs[1][...] + (e2 / z) * og_refs[2][...]


def _attn_prompt(qk3, h3):
    b, t, _ = qk3.shape
    tq = ATT_TILE
    assert t % tq == 0
    nh = N_GROUPS * G_HEADS
    voff = (4 * A_WIDTH + 2 * ATT_WIDTH) // HEAD_DIM
    cur = lambda off: pl.BlockSpec((None, tq, HEAD_DIM), lambda bi, ji, ti, off=off: (bi, ti, off + ji))
    prev = lambda off: pl.BlockSpec((None, tq, HEAD_DIM),
                                    lambda bi, ji, ti, off=off: (bi, jnp.maximum(ti - 1, 0), off + ji))
    in_specs, args = [], []
    for g in range(N_GROUPS):
        in_specs += [cur(g * G_HEADS), cur(nh + g * G_HEADS), prev(nh + g * G_HEADS),
                     cur(voff + g * G_HEADS), prev(voff + g * G_HEADS)]
        args += [qk3, qk3, qk3, h3, h3]
    return pl.pallas_call(
        functools.partial(_attn_kernel, tq=tq),
        grid=(b, G_HEADS, t // tq),
        in_specs=in_specs,
        out_specs=pl.BlockSpec((None, tq, HEAD_DIM), lambda bi, ji, ti: (bi, ti, ji)),
        out_shape=jax.ShapeDtypeStruct((b, t, G_HEADS * HEAD_DIM), F32),
        scratch_shapes=[pltpu.VMEM((tq, HEAD_DIM), F32)] * (2 * N_GROUPS),
        compiler_params=_cparams("parallel", "parallel", "arbitrary"),
        name="attn_prompt",
    )(*args)


def _attn_sample_kernel(*refs, tp, t_valid):
    o_ref = refs[12]
    scale = HEAD_DIM ** -0.5
    outs, lses = [], []
    for g, (win, dil) in enumerate(ATT_GROUPS):
        c_ref, q_ref, k_ref, v_ref = refs[4 * g:4 * g + 4]
        n_back = win // dil
        w = c_ref.shape[1]
        qb = q_ref[...].astype(BF16)
        s_b = lax.dot_general(qb, c_ref[0].astype(BF16), NT_DIMS, preferred_element_type=F32) * scale
        s_n = lax.dot_general(qb, k_ref[...].astype(BF16), NT_DIMS, preferred_element_type=F32) * scale
        tq_b = lax.broadcasted_iota(jnp.int32, (tp, w), 0)
        rb = lax.broadcasted_iota(jnp.int32, (tp, w), 1)
        delta = w + tq_b - rb
        ok_b = jnp.logical_and(delta % dil == 0, delta // dil <= n_back)
        tq_n = lax.broadcasted_iota(jnp.int32, (tp, tp), 0)
        tn = lax.broadcasted_iota(jnp.int32, (tp, tp), 1)
        dn = tq_n - tn
        ok_n = (dn >= 0) & (dn % dil == 0) & (dn // dil <= n_back) & (tn < t_valid)
        s_b = jnp.where(ok_b, s_b, -jnp.inf)
        s_n = jnp.where(ok_n, s_n, -jnp.inf)
        o, lse = _softmax_block([s_b, s_n], [c_ref[1], v_ref[...]])
        outs.append(o)
        lses.append(lse)
    m = jnp.maximum(jnp.maximum(lses[0], lses[1]), lses[2])
    es = [jnp.exp(l - m) for l in lses]
    z = es[0] + es[1] + es[2]
    o_ref[...] = (es[0] / z) * outs[0] + (es[1] / z) * outs[1] + (es[2] / z) * outs[2]


def _attn_sample(qk3, h3, caches, t_valid):
    b, tp, _ = qk3.shape
    nh = N_GROUPS * G_HEADS
    voff = (4 * A_WIDTH + 2 * ATT_WIDTH) // HEAD_DIM
    row = lambda off: pl.BlockSpec((None, tp, HEAD_DIM), lambda bi, ji, off=off: (bi, 0, off + ji))
    in_specs, args = [], []
    for g in range(N_GROUPS):
        w = caches[g].shape[3]
        in_specs += [pl.BlockSpec((None, 2, None, w, HEAD_DIM), lambda bi, ji: (bi, 0, ji, 0, 0)),
                     row(g * G_HEADS), row(nh + g * G_HEADS), row(voff + g * G_HEADS)]
        args += [caches[g], qk3, qk3, h3]
    return pl.pallas_call(
        functools.partial(_attn_sample_kernel, tp=tp, t_valid=t_valid),
        grid=(b, G_HEADS),
        in_specs=in_specs,
        out_specs=pl.BlockSpec((None, tp, HEAD_DIM), lambda bi, ji: (bi, 0, ji)),
        out_shape=jax.ShapeDtypeStruct((b, tp, G_HEADS * HEAD_DIM), F32),
        compiler_params=_cparams("parallel", "arbitrary"),
        name="attn_sample",
    )(*args)


def _out_proj_kernel(oa_ref, ob_ref, ga0_ref, ga1_ref, gb0_ref, gb1_ref, x_ref, g1_ref, sc_ref, sh_ref,
                     nw_ref, wa_ref, wb_ref, wo_ref, x1_ref, n2_ref):
    ya = jnp.dot(oa_ref[...].astype(BF16), wa_ref[...], preferred_element_type=F32)
    yb = jnp.dot(ob_ref[...].astype(BF16), wb_ref[...], preferred_element_type=F32)
    gate_a = _sigmoid(jnp.concatenate([ga0_ref[...], ga1_ref[...]], axis=-1))
    gate_b = _sigmoid(jnp.concatenate([gb0_ref[...], gb1_ref[...]], axis=-1))
    mix = jnp.dot((gate_a * ya + gate_b * yb).astype(BF16), wo_ref[...], preferred_element_type=F32)
    x1 = x_ref[...] + g1_ref[...] * mix
    x1_ref[...] = x1
    n2_ref[...] = (_rms(x1) * nw_ref[...]) * (1.0 + sc_ref[...]) + sh_ref[...]


def _out_proj(oa2, ob2, h2, x2, g1, sc2, sh2, nw2, wa_bf, wb_bf, wo_bf, rows_per_group):
    n, d = x2.shape
    tm = min(512, n, rows_per_group if g1.ndim == 3 else n)
    gw = 512
    goff = (4 * A_WIDTH + 3 * ATT_WIDTH) // gw
    rows = lambda width: pl.BlockSpec((tm, width), lambda i: (i, 0))
    gcol = lambda k: pl.BlockSpec((tm, gw), lambda i, k=k: (i, goff + k))
    full = lambda a: pl.BlockSpec(a.shape, lambda i: (0, 0))
    mod = lambda a: _mod_spec(a, rows_per_group, tm, 1)
    return pl.pallas_call(
        _out_proj_kernel,
        grid=(n // tm,),
        in_specs=[rows(A_WIDTH), rows(G_HEADS * HEAD_DIM), gcol(0), gcol(1), gcol(2), gcol(3), rows(d),
                  mod(g1), mod(sc2), mod(sh2), pl.BlockSpec((1, d), lambda i: (0, 0)),
                  full(wa_bf), full(wb_bf), full(wo_bf)],
        out_specs=[rows(d), rows(d)],
        out_shape=[jax.ShapeDtypeStruct((n, d), F32), jax.ShapeDtypeStruct((n, d), F32)],
        compiler_params=_cparams("parallel"),
        name="out_proj",
    )(oa2, ob2, h2, h2, h2, h2, x2, g1, sc2, sh2, nw2.reshape(1, d), wa_bf, wb_bf, wo_bf)


def _topk_rows(s, k, extra=None):
    rows = s.shape[0]
    ridx = lax.broadcasted_iota(jnp.int32, s.shape, 0).astype(F32)
    vals, idxs, extras = [], [], []
    for _ in range(k):
        m = jnp.max(s, axis=0, keepdims=True)
        idx = jnp.min(jnp.where(s == m, ridx, float(rows)), axis=0, keepdims=True)
        hit = ridx == idx
        vals.append(m)
        idxs.append(idx.astype(jnp.int32))
        if extra is not None:
            extras.append(jnp.sum(jnp.where(hit, extra, 0), axis=0, keepdims=True))
        s = jnp.where(hit, -jnp.inf, s)
    return vals, idxs, extras


CAND_CELLS = tuple((a, b) for a in range(P_TOPK) for b in range(P_TOPK) if (a + 1) * (b + 1) <= P_TOPK)
CAND_PAD = -len(CAND_CELLS) % SUBLANES
SEL_TILE = 256


def _peer_sel_kernel(n2_ref, wq_ref, qn_ref, k1_ref, k2_ref, e_ref, g_ref, q_ref, *, tm):
    q = jnp.dot(n2_ref[...].astype(BF16), wq_ref[...], preferred_element_type=F32)
    q_ref[...] = _rms(q) * qn_ref[...]
    hi = lax.Precision.HIGHEST

    tr = min(SEL_TILE, tm)

    def tile(ti, carry):
        q = q_ref[pl.ds(pl.multiple_of(ti * tr, tr), tr), :]
        s1 = lax.dot_general(k1_ref[...], q[:, :P_HALF], NT_DIMS, preferred_element_type=F32, precision=hi)
        s2 = lax.dot_general(k2_ref[...], q[:, P_HALF:], NT_DIMS, preferred_element_type=F32, precision=hi)
        v1, i1, _ = _topk_rows(s1, P_TOPK)
        v2, i2, _ = _topk_rows(s2, P_TOPK)
        cand = [v1[a] + v2[b] for a, b in CAND_CELLS]
        cidx = [i1[a] * N_KEYS + i2[b] for a, b in CAND_CELLS]
        cand += [jnp.full_like(cand[0], -jnp.inf)] * CAND_PAD
        cidx += [jnp.zeros_like(cidx[0])] * CAND_PAD
        sc, _, eidx = _topk_rows(jnp.concatenate(cand, axis=0), P_TOPK, extra=jnp.concatenate(cidx, axis=0))
        sc = jnp.concatenate(sc, axis=0)
        p = jnp.exp(sc - jnp.max(sc, axis=0, keepdims=True))
        cols = pl.ds(pl.multiple_of(ti * tr, tr), tr)
        g_ref[:, cols] = p / jnp.sum(p, axis=0, keepdims=True)
        e_ref[:, cols] = jnp.concatenate(eidx, axis=0)
        return carry

    lax.fori_loop(0, tm // tr, tile, 0)


def _peer_sel(n2, wq_bf, qn_w, k1, k2):
    n, d = n2.shape
    tm = min(256, n)
    assert tm % min(SEL_TILE, tm) == 0
    return pl.pallas_call(
        functools.partial(_peer_sel_kernel, tm=tm),
        grid=(n // tm, P_HEADS),
        in_specs=[pl.BlockSpec((tm, d), lambda i, h: (i, 0)),
                  pl.BlockSpec((d, P_QDIM), lambda i, h: (0, h)),
                  pl.BlockSpec((1, P_QDIM), lambda i, h: (0, 0)),
                  pl.BlockSpec((None, N_KEYS, P_HALF), lambda i, h: (h, 0, 0)),
                  pl.BlockSpec((None, N_KEYS, P_HALF), lambda i, h: (h, 0, 0))],
        out_specs=[pl.BlockSpec((None, P_TOPK, tm), lambda i, h: (h, 0, i)),
                   pl.BlockSpec((None, P_TOPK, tm), lambda i, h: (h, 0, i))],
        out_shape=[jax.ShapeDtypeStruct((P_HEADS, P_TOPK, n), jnp.int32),
                   jax.ShapeDtypeStruct((P_HEADS, P_TOPK, n), F32)],
        scratch_shapes=[pltpu.VMEM((tm, P_QDIM), F32)],
        compiler_params=_cparams("parallel", "arbitrary"),
        name="peer_sel",
    )(n2, wq_bf, qn_w.reshape(1, P_QDIM), k1, k2)


HI_HALF = 0xFFFF0000
ROWS_PER_EXPERT = 4


HALF_SLOTS = P_PAIRS // 2


def _pack_table(tab):
    e, d = tab.shape
    assert d == 2 * ROWS_PER_EXPERT * LANES
    bits = lax.bitcast_convert_type(tab.astype(BF16), jnp.uint16).astype(jnp.uint32)
    bits = bits.reshape(e, 2, ROWS_PER_EXPERT, LANES)
    packed = (bits[:, 0] | (bits[:, 1] << 16)).reshape(e * ROWS_PER_EXPERT, LANES)
    return jnp.pad(packed, ((ROWS_PER_EXPERT, ROWS_PER_EXPERT), (0, 0)))


def _table_rows(e_rm):
    first = (e_rm + 1) * ROWS_PER_EXPERT
    slot = lax.broadcasted_iota(jnp.int32, e_rm.shape, 1)
    return jnp.where(slot < HALF_SLOTS, first, first - ROWS_PER_EXPERT)


def _pair_halves(tab_ref, row_a, row_b, top):
    tile = jnp.where(top, tab_ref[pl.ds(row_a, SUBLANES), :], tab_ref[pl.ds(row_b, SUBLANES), :])
    lo = lax.bitcast_convert_type(tile << jnp.uint32(16), F32)
    hi = lax.bitcast_convert_type(tile & jnp.uint32(HI_HALF), F32)
    return lo, hi


def _top_half():
    return lax.broadcasted_iota(jnp.int32, (SUBLANES, LANES), 0) < ROWS_PER_EXPERT


def _gelu(x):
    return 0.5 * x * (1.0 + jnp.tanh(0.7978845608028654 * (x + 0.044715 * x * x * x)))


def _smem_block_copies(srcs, dsts, sem, tb):
    rows = pl.ds(pl.program_id(0) * (tb * P_PAIRS), tb * P_PAIRS)
    return [pltpu.make_async_copy(src.at[rows], dst, sem.at[k]) for k, (src, dst) in enumerate(zip(srcs, dsts))]


TILE_CHUNK = 16
N_CHUNKS = HALF_SLOTS // TILE_CHUNK
FOLD = ROWS_PER_EXPERT // N_CHUNKS
U_GROUP = 8


def _stage_sublane(sum_ref, k):
    return sum_ref[pl.ds(k, HALF_SLOTS, stride=SUBLANES), :]


def _peer_u_row(tab_ref, idx_smem, x_ref, t, fill_ref, sum_ref):
    x_row = x_ref[pl.ds(t, 1), :]
    pieces = [x_row[:, s * LANES:(s + 1) * LANES] for s in range(SUBLANES)]
    x_lo = jnp.concatenate(pieces[:ROWS_PER_EXPERT] * 2, axis=0)
    x_hi = jnp.concatenate(pieces[ROWS_PER_EXPERT:] * 2, axis=0)
    top = _top_half()

    def chunk(c, parts):
        q0 = c * TILE_CHUNK
        base = t * P_PAIRS + q0
        for p in range(TILE_CHUNK):
            lo, hi = _pair_halves(tab_ref, idx_smem[base + p], idx_smem[base + HALF_SLOTS + p], top)
            fill_ref[pl.ds(pl.multiple_of((q0 + p) * SUBLANES, SUBLANES), SUBLANES), :] = lo * x_lo + hi * x_hi
        first, second = parts
        for k in range(FOLD):
            first = first + _stage_sublane(sum_ref, c * FOLD + k)
            second = second + _stage_sublane(sum_ref, ROWS_PER_EXPERT + c * FOLD + k)
        return first, second

    zero = jnp.zeros((HALF_SLOTS, LANES), F32)
    return jnp.concatenate(lax.fori_loop(0, N_CHUNKS, chunk, (zero, zero)), axis=0)


def _peer_u_kernel(idx_hbm, x_ref, g_ref, tab_ref, w_ref, idx_smem, sem, stage_a, stage_b, part_ref, act_ref, *, tb):
    copies = _smem_block_copies((idx_hbm,), (idx_smem,), sem, tb)
    for cp in copies:
        cp.start()
    for cp in copies:
        cp.wait()
    stage_b[...] = jnp.zeros(stage_b.shape, F32)

    def two_rows(j, carry):
        t0 = 2 * j
        part_ref[jnp.maximum(t0 - 1, 0)] = _peer_u_row(tab_ref, idx_smem, x_ref, t0, stage_a, stage_b)
        part_ref[t0] = _peer_u_row(tab_ref, idx_smem, x_ref, t0 + 1, stage_b, stage_a)
        return carry

    lax.fori_loop(0, tb // 2, two_rows, 0)
    halves = []
    for h in range(2):
        s = _stage_sublane(stage_b, h * ROWS_PER_EXPERT)
        for k in range(1, ROWS_PER_EXPERT):
            s = s + _stage_sublane(stage_b, h * ROWS_PER_EXPERT + k)
        halves.append(s)
    part_ref[tb - 1] = jnp.concatenate(halves, axis=0)

    eye = (lax.broadcasted_iota(jnp.int32, (P_PAIRS, LANES), 0)
           == lax.broadcasted_iota(jnp.int32, (P_PAIRS, LANES), 1))

    def lane_sums(gi, carry):
        rows = []
        for k in range(U_GROUP):
            col = jnp.sum(part_ref[gi * U_GROUP + k], axis=1, keepdims=True)
            rows.append(jnp.sum(jnp.where(eye, col, 0.0), axis=0, keepdims=True))
        act_ref[pl.ds(pl.multiple_of(gi * U_GROUP, U_GROUP), U_GROUP), :] = jnp.concatenate(rows, axis=0)
        return carry

    lax.fori_loop(0, tb // U_GROUP, lane_sums, 0)
    w_ref[...] = g_ref[...] * _gelu(act_ref[...])


def _peer_u(idx, x2, g2d, tab):
    n, d = x2.shape
    tb = min(128, n)
    assert tb % U_GROUP == 0 and n % tb == 0
    return pl.pallas_call(
        functools.partial(_peer_u_kernel, tb=tb),
        grid=(n // tb,),
        in_specs=[pl.BlockSpec(memory_space=pl.ANY),
                  pl.BlockSpec((tb, d), lambda i: (i, 0)),
                  pl.BlockSpec((tb, P_PAIRS), lambda i: (i, 0)),
                  pl.BlockSpec(tab.shape, lambda i: (0, 0), pipeline_mode=pl.Buffered(1))],
        out_specs=pl.BlockSpec((tb, P_PAIRS), lambda i: (i, 0)),
        out_shape=jax.ShapeDtypeStruct((n, P_PAIRS), F32),
        scratch_shapes=[pltpu.SMEM((tb * P_PAIRS,), jnp.int32), pltpu.SemaphoreType.DMA((1,)),
                        pltpu.VMEM((HALF_SLOTS * SUBLANES, LANES), F32),
                        pltpu.VMEM((HALF_SLOTS * SUBLANES, LANES), F32),
                        pltpu.VMEM((tb, P_PAIRS, LANES), F32),
                        pltpu.VMEM((tb, P_PAIRS), F32)],
        compiler_params=_cparams("arbitrary"),
        name="peer_u",
    )(idx, x2, g2d, tab)


def _peer_v_kernel(idx_hbm, w_hbm, x1_ref, g2_ref, tab_ref, y_ref, idx_smem, w_smem, sem, out_ref, *, tb):
    copies = _smem_block_copies((idx_hbm, w_hbm), (idx_smem, w_smem), sem, tb)
    for cp in copies:
        cp.start()
    for cp in copies:
        cp.wait()
    nacc = 2
    top = _top_half()

    def tok(t, carry):
        def chunk(c, accs):
            los, his = list(accs[:nacc]), list(accs[nacc:])
            base = t * P_PAIRS + c * TILE_CHUNK
            for p in range(TILE_CHUNK):
                a, b = base + p, base + HALF_SLOTS + p
                lo, hi = _pair_halves(tab_ref, idx_smem[a], idx_smem[b], top)
                w = jnp.where(top, w_smem[a], w_smem[b])
                los[p % nacc] = los[p % nacc] + w * lo
                his[p % nacc] = his[p % nacc] + w * hi
            return tuple(los + his)

        zero = jnp.zeros((SUBLANES, LANES), F32)
        accs = lax.fori_loop(0, N_CHUNKS, chunk, (zero,) * (2 * nacc))
        lo, hi = accs[0] + accs[1], accs[2] + accs[3]
        tile = jnp.concatenate([lo[:ROWS_PER_EXPERT] + lo[ROWS_PER_EXPERT:],
                                hi[:ROWS_PER_EXPERT] + hi[ROWS_PER_EXPERT:]], axis=0)
        out_ref[pl.ds(t, 1), :] = jnp.concatenate([tile[s:s + 1, :] for s in range(SUBLANES)], axis=1)
        return carry

    lax.fori_loop(0, tb, tok, 0)
    y_ref[...] = x1_ref[...] + g2_ref[...] * out_ref[...]


def _peer_v(idx, w2d, x1, g2, tab, rows_per_group):
    n, d = x1.shape
    tb = min(256, n, rows_per_group if g2.ndim == 3 else n)
    return pl.pallas_call(
        functools.partial(_peer_v_kernel, tb=tb),
        grid=(n // tb,),
        in_specs=[pl.BlockSpec(memory_space=pl.ANY),
                  pl.BlockSpec(memory_space=pl.ANY),
                  pl.BlockSpec((tb, d), lambda i: (i, 0)),
                  _mod_spec(g2, rows_per_group, tb, 1),
                  pl.BlockSpec(tab.shape, lambda i: (0, 0), pipeline_mode=pl.Buffered(1))],
        out_specs=pl.BlockSpec((tb, d), lambda i: (i, 0)),
        out_shape=jax.ShapeDtypeStruct((n, d), F32),
        scratch_shapes=[pltpu.SMEM((tb * P_PAIRS,), jnp.int32), pltpu.SMEM((tb * P_PAIRS,), F32),
                        pltpu.SemaphoreType.DMA((2,)), pltpu.VMEM((tb, d), F32)],
        compiler_params=_cparams("arbitrary"),
        name="peer_v",
    )(idx, w2d, x1, g2, tab)


def _group_mod(m):
    return m[:, None, :]


def _layer(x, mod, pos, lb, s0, kv_bufs, wts):
    (norm1_w, norm2_w, w_in_bf, hgrn_norm_w, qk_w, wa_bf, wb_bf, wo_bf, wq_bf, peer_qn_w,
     peer_k1, peer_k2, u_tab, v_tab) = wts
    b, t, d = x.shape
    n = b * t
    sample = kv_bufs is not None
    sh1, sc1, g1, sh2, sc2, g2 = jnp.split(mod, 6, axis=-1)
    if sample:
        per_row = lambda m: jnp.repeat(m, t, axis=0)
        sh1, sc1, g1, sh2, sc2, g2 = map(per_row, (sh1, sc1, g1, sh2, sc2, g2))
    else:
        sh1, sc1, g1, sh2, sc2, g2 = map(_group_mod, (sh1, sc1, g1, sh2, sc2, g2))
    x2 = x.reshape(n, d)
    h2 = _in_proj(x2, sc1, sh1, norm1_w, w_in_bf, t)
    h3 = h2.reshape(b, t, IN_COLS)

    if sample:
        tp = HGRN_SUB
        h3p = jnp.pad(h3, ((0, 0), (0, tp - t), (0, 0)))
        pos_p = pos[0] + jnp.arange(tp)
        oa, s_new = _hgrn(h3p, lb, s0, hgrn_norm_w, t_valid=t)
        cos_t, sin_t = _rope_tables(pos_p)
        qk3 = _qk_rope(h3p, qk_w, cos_t, sin_t)
        ob = _attn_sample(qk3, h3p, kv_bufs, t)
        oa, ob, qk3 = oa[:, :t], ob[:, :t], qk3[:, :t]
    else:
        oa, s_new = _hgrn(h3, lb, s0, hgrn_norm_w)
        cos_t, sin_t = _rope_tables(pos)
        qk3 = _qk_rope(h3, qk_w, cos_t, sin_t)
        ob = _attn_prompt(qk3, h3)

    voff = 4 * A_WIDTH + 2 * ATT_WIDTH
    gw = G_HEADS * HEAD_DIM
    new_kv = []
    for gi, (win, _) in enumerate(ATT_GROUPS):
        keep = t if sample else min(win, t)
        head_major = lambda a: a.reshape(b, keep, G_HEADS, HEAD_DIM).transpose(0, 2, 1, 3)
        kg = head_major(qk3[:, t - keep:t, ATT_WIDTH + gi * gw:ATT_WIDTH + (gi + 1) * gw])
        vg = head_major(h3[:, t - keep:t, voff + gi * gw:voff + (gi + 1) * gw])
        new_kv.append(jnp.stack([kg, vg], axis=1))

    x1, n2 = _out_proj(oa.reshape(n, A_WIDTH), ob.reshape(n, G_HEADS * HEAD_DIM), h2, x2, g1, sc2, sh2,
                       norm2_w, wa_bf, wb_bf, wo_bf, t)
    e_t, g_t = _peer_sel(n2, wq_bf, peer_qn_w, peer_k1, peer_k2)
    idx = _table_rows(e_t.reshape(P_PAIRS, n).T).reshape(n * P_PAIRS)
    g2d = g_t.reshape(P_PAIRS, n).T
    w = _peer_u(idx, n2, g2d, u_tab)
    y2 = _peer_v(idx, w.reshape(n * P_PAIRS), x1, g2, v_tab, t)
    return y2.reshape(b, t, d), new_kv, s_new


def kernel(x_prompt, x_sample, cache_kv_g1, cache_kv_g2, cache_kv_g3, state_hgrn, c_prompt, c_sample, w_ada, b_ada, norm1_w, norm2_w, w_in, lb_logits, hgrn_norm_w, q_norm_w, k_norm_w, w_branch_a, w_branch_b, w_out, peer_wq, peer_qn_w, peer_k1, peer_k2, peer_u, peer_v):
    depth = w_ada.shape[0]
    bp, tp_len, _ = x_prompt.shape
    bs, ts_len, _ = x_sample.shape
    pos_p = jnp.arange(tp_len)
    pos_s = PAST_LEN + jnp.arange(ts_len)
    lb_all = jnp.cumsum(jax.nn.softmax(lb_logits.astype(F32), axis=0), axis=0)
    caches = (cache_kv_g1, cache_kv_g2, cache_kv_g3)
    c_all = jnp.concatenate([c_prompt, c_sample], axis=0)
    yp, ys = x_prompt, x_sample
    kvp, kvs = ([], [], []), ([], [], [])
    sp_list, ss_list = [], []
    for l in range(depth):
        mod = _ada(c_all, w_ada[l], b_ada[l])
        wts = (norm1_w[l], norm2_w[l], w_in[l].astype(BF16), hgrn_norm_w[l],
               jnp.stack([q_norm_w[l], k_norm_w[l]])[:, None, :],
               w_branch_a[l].astype(BF16), w_branch_b[l].astype(BF16), w_out[l].astype(BF16),
               peer_wq[l].astype(BF16), peer_qn_w[l], peer_k1[l], peer_k2[l],
               _pack_table(peer_u[l]), _pack_table(peer_v[l]))
        lb = lb_all[l].reshape(A_HEADS, A_DK)
        s0p = jnp.zeros((bp, A_HEADS, A_DK, A_DK), F32)
        yp, nkv_p, sp = _layer(yp, mod[:bp], pos_p, lb, s0p, None, wts)
        ys, nkv_s, ss = _layer(ys, mod[bp:], pos_s, lb, state_hgrn[l],
                               tuple(c[l] for c in caches), wts)
        for gi in range(N_GROUPS):
            kvp[gi].append(nkv_p[gi])
            kvs[gi].append(nkv_s[gi])
        sp_list.append(sp)
        ss_list.append(ss)
    return (yp, ys, jnp.stack(kvp[0]), jnp.stack(kvp[1]), jnp.stack(kvp[2]), jnp.stack(sp_list),
            jnp.stack(kvs[0]), jnp.stack(kvs[1]), jnp.stack(kvs[2]), jnp.stack(ss_list))
```

```python
import functools

import jax
import jax.numpy as jnp
from jax import lax
from jax.experimental import pallas as pl
from jax.experimental.pallas import tpu as pltpu

F32 = jnp.float32
BF16 = jnp.bfloat16

D_MODEL = 1024
PAST_LEN = 16384
A_HEADS = 4
A_DK = 128
A_WIDTH = A_HEADS * A_DK
ATT_GROUPS = ((128, 1), (512, 4), (2048, 16))
N_GROUPS = 3
G_HEADS = 4
HEAD_DIM = 128
ATT_WIDTH = N_GROUPS * G_HEADS * HEAD_DIM
ATT_BLOCK = 128
ROT_DIM = HEAD_DIM // 4
ROPE_THETA = 500000.0
IN_COLS = 4 * A_WIDTH + 3 * ATT_WIDTH + 2 * D_MODEL
P_HEADS = 8
N_KEYS = 128
P_QDIM = 256
P_HALF = P_QDIM // 2
P_TOPK = 16
P_PAIRS = P_HEADS * P_TOPK
EPS = 1e-6

LANES = 128
SUBLANES = 8
HGRN_SUB = 16
ATT_TILE = ATT_BLOCK * ATT_GROUPS[-1][1]
ATT_UNROLL = 4
VMEM_LIMIT = 56 * 1024 * 1024

NT_DIMS = (((1,), (1,)), ((), ()))
TN_DIMS = (((0,), (0,)), ((), ()))


def _cparams(*sem):
    return pltpu.CompilerParams(dimension_semantics=sem, vmem_limit_bytes=VMEM_LIMIT)


def _rms(x):
    return x * lax.rsqrt(jnp.mean(x * x, axis=-1, keepdims=True) + EPS)


def _sigmoid(x):
    return 1.0 / (1.0 + jnp.exp(-x))


def _mod_spec(arr, rows_per_group, tm, ngrid):
    if arr.ndim == 3:
        if ngrid == 2:
            return pl.BlockSpec((None, 1, arr.shape[-1]), lambda i, j: (i * tm // rows_per_group, 0, 0))
        return pl.BlockSpec((None, 1, arr.shape[-1]), lambda i: (i * tm // rows_per_group, 0, 0))
    if ngrid == 2:
        return pl.BlockSpec((tm, arr.shape[-1]), lambda i, j: (i, 0))
    return pl.BlockSpec((tm, arr.shape[-1]), lambda i: (i, 0))


def _ada_kernel(c_ref, w_ref, b_ref, o_ref):
    c = c_ref[...]
    s = c * _sigmoid(c)
    o_ref[...] = jnp.dot(s, w_ref[...], preferred_element_type=F32) + b_ref[...]


def _ada(c, w, b):
    rows, d = c.shape
    n = -(-rows // SUBLANES) * SUBLANES
    c = jnp.pad(c, ((0, n - rows), (0, 0)))
    cols = w.shape[1]
    tn = 1024
    out = pl.pallas_call(
        _ada_kernel,
        grid=(cols // tn,),
        in_specs=[pl.BlockSpec((n, d), lambda j: (0, 0)),
                  pl.BlockSpec((d, tn), lambda j: (0, j)),
                  pl.BlockSpec((1, tn), lambda j: (0, j))],
        out_specs=pl.BlockSpec((n, tn), lambda j: (0, j)),
        out_shape=jax.ShapeDtypeStruct((n, cols), F32),
        compiler_params=_cparams("arbitrary"),
        name="ada",
    )(c, w, b.reshape(1, cols))
    return out[:rows]


def _in_proj_kernel(x_ref, sc_ref, sh_ref, nw_ref, w_ref, o_ref, xn_ref):
    @pl.when(pl.program_id(1) == 0)
    def _():
        y = _rms(x_ref[...]) * nw_ref[...]
        y = y * (1.0 + sc_ref[...]) + sh_ref[...]
        xn_ref[...] = y.astype(BF16)

    o_ref[...] = jnp.dot(xn_ref[...], w_ref[...], preferred_element_type=F32)


def _in_proj(x2, sc, sh, nw, w_bf, rows_per_group):
    n, d = x2.shape
    cols = w_bf.shape[1]
    tm = min(1024, n, rows_per_group if sc.ndim == 3 else n)
    tn = cols // 4
    assert tn % LANES == 0
    return pl.pallas_call(
        _in_proj_kernel,
        grid=(n // tm, cols // tn),
        in_specs=[pl.BlockSpec((tm, d), lambda i, j: (i, 0)),
                  _mod_spec(sc, rows_per_group, tm, 2),
                  _mod_spec(sh, rows_per_group, tm, 2),
                  pl.BlockSpec((1, d), lambda i, j: (0, 0)),
                  pl.BlockSpec((d, tn), lambda i, j: (0, j))],
        out_specs=pl.BlockSpec((tm, tn), lambda i, j: (i, j)),
        out_shape=jax.ShapeDtypeStruct((n, cols), F32),
        scratch_shapes=[pltpu.VMEM((tm, d), BF16)],
        compiler_params=_cparams("parallel", "arbitrary"),
        name="in_proj",
    )(x2, sc, sh, nw.reshape(1, d), w_bf)


def _hgrn_kernel(q_ref, f_ref, i_ref, g_ref, lb_ref, s0_ref, nw_ref, o_ref, sout_ref,
                 st_ref, b_ref, k_ref, *, tt, sub, t_valid):
    t = pl.program_id(1)

    @pl.when(t == 0)
    def _():
        for h in range(A_HEADS):
            st_ref[h] = s0_ref[h].T

    lb = lb_ref[...]
    f = lb + (1.0 - lb) * _sigmoid(f_ref[...])
    logf = jnp.log(f)
    kk = 1.0 - f
    if t_valid is not None:
        live = lax.broadcasted_iota(jnp.int32, (tt, 1), 0) < t_valid
        logf = jnp.where(live, logf, 0.0)
        kk = jnp.where(live, kk, 0.0)
    r = lax.broadcasted_iota(jnp.int32, (tt, tt), 0)
    c = lax.broadcasted_iota(jnp.int32, (tt, tt), 1)
    tri = ((r // sub == c // sub) & (c <= r)).astype(F32)
    b_ref[...] = jnp.dot(tri, logf, preferred_element_type=F32, precision=lax.Precision.HIGHEST)
    k_ref[...] = kk

    row = lax.broadcasted_iota(jnp.int32, (sub, 1), 0)
    nw = nw_ref[...]

    def body(i, carry):
        sl = pl.ds(pl.multiple_of(i * sub, sub), sub)
        for h in range(A_HEADS):
            cols = slice(h * A_DK, (h + 1) * A_DK)
            bq = b_ref[sl, cols]
            kq = k_ref[sl, cols]
            qq = q_ref[sl, cols]
            vv = i_ref[sl, cols]
            st = st_ref[h]
            qe = qq * jnp.exp(bq)
            o = lax.dot_general(qe.astype(BF16), st.astype(BF16), NT_DIMS, preferred_element_type=F32)
            for s in range(sub):
                dd = jnp.where(row >= s, bq - bq[s:s + 1, :], -jnp.inf)
                m = qq * kq[s:s + 1, :] * jnp.exp(dd)
                o = o + jnp.sum(m, axis=-1, keepdims=True) * vv[s:s + 1, :]
            bl = bq[sub - 1:sub, :]
            kd = kq * jnp.exp(bl - bq)
            st_ref[h] = st * jnp.exp(bl) + lax.dot_general(vv.astype(BF16), kd.astype(BF16), TN_DIMS,
                                                           preferred_element_type=F32)
            gq = g_ref[sl, cols]
            o_ref[sl, cols] = _rms(o) * nw * (gq * _sigmoid(gq))
        return carry

    lax.fori_loop(0, tt // sub, body, 0)

    @pl.when(t == pl.num_programs(1) - 1)
    def _():
        for h in range(A_HEADS):
            sout_ref[h] = st_ref[h].T


def _hgrn(h3, lb, s0, nw, t_valid=None):
    b, t, _ = h3.shape
    tt = min(256, t)
    sub = min(HGRN_SUB, tt)
    blk = lambda off: pl.BlockSpec((None, tt, A_WIDTH), lambda bi, ti: (bi, ti, off))
    return pl.pallas_call(
        functools.partial(_hgrn_kernel, tt=tt, sub=sub, t_valid=t_valid),
        grid=(b, t // tt),
        in_specs=[blk(0), blk(1), blk(2), blk(3),
                  pl.BlockSpec((1, A_WIDTH), lambda bi, ti: (0, 0)),
                  pl.BlockSpec((None, A_HEADS, A_DK, A_DK), lambda bi, ti: (bi, 0, 0, 0)),
                  pl.BlockSpec((1, A_DK), lambda bi, ti: (0, 0))],
        out_specs=[pl.BlockSpec((None, tt, A_WIDTH), lambda bi, ti: (bi, ti, 0)),
                   pl.BlockSpec((None, A_HEADS, A_DK, A_DK), lambda bi, ti: (bi, 0, 0, 0))],
        out_shape=[jax.ShapeDtypeStruct((b, t, A_WIDTH), F32),
                   jax.ShapeDtypeStruct((b, A_HEADS, A_DK, A_DK), F32)],
        scratch_shapes=[pltpu.VMEM((A_HEADS, A_DK, A_DK), F32), pltpu.VMEM((tt, A_WIDTH), F32),
                        pltpu.VMEM((tt, A_WIDTH), F32)],
        compiler_params=_cparams("parallel", "arbitrary"),
        name="hgrn",
    )(h3, h3, h3, h3, lb.reshape(1, A_WIDTH), s0, nw.reshape(1, A_DK))


ROPE_HEADS = 4


def _qk_rope_kernel(h_ref, w_ref, cos_ref, sin_ref, o_ref):
    half = ROT_DIM // 2
    lane = lax.broadcasted_iota(jnp.int32, cos_ref.shape, 1)
    for k in range(ROPE_HEADS):
        cols = slice(k * HEAD_DIM, (k + 1) * HEAD_DIM)
        y = _rms(h_ref[:, cols]) * w_ref[...]
        partner = jnp.where(lane < half, pltpu.roll(y, LANES - half, axis=1), pltpu.roll(y, half, axis=1))
        o_ref[:, cols] = y * cos_ref[...] + partner * sin_ref[...]


def _rope_tables(pos):
    inv = ROPE_THETA ** (-jnp.arange(0, ROT_DIM, 2, dtype=F32) / ROT_DIM)
    ang = pos.astype(F32)[:, None] * inv[None, :]
    cos, sin = jnp.cos(ang), jnp.sin(ang)
    rest = HEAD_DIM - ROT_DIM
    cos_t = jnp.concatenate([cos, cos, jnp.ones((pos.shape[0], rest), F32)], axis=-1)
    sin_t = jnp.concatenate([-sin, sin, jnp.zeros((pos.shape[0], rest), F32)], axis=-1)
    return cos_t, sin_t


def _qk_rope(h3, qk_w, cos_t, sin_t):
    b, t, _ = h3.shape
    tt = min(1024, t)
    width = ROPE_HEADS * HEAD_DIM
    nq = ATT_WIDTH // width
    off = 4 * A_WIDTH // width
    return pl.pallas_call(
        _qk_rope_kernel,
        grid=(b, t // tt, 2 * nq),
        in_specs=[pl.BlockSpec((None, tt, width), lambda bi, ti, hi: (bi, ti, off + hi)),
                  pl.BlockSpec((None, 1, HEAD_DIM), lambda bi, ti, hi: (hi // nq, 0, 0)),
                  pl.BlockSpec((tt, HEAD_DIM), lambda bi, ti, hi: (ti, 0)),
                  pl.BlockSpec((tt, HEAD_DIM), lambda bi, ti, hi: (ti, 0))],
        out_specs=pl.BlockSpec((None, tt, width), lambda bi, ti, hi: (bi, ti, hi)),
        out_shape=jax.ShapeDtypeStruct((b, t, 2 * ATT_WIDTH), F32),
        compiler_params=_cparams("parallel", "parallel", "arbitrary"),
        name="qk_rope",
    )(h3, qk_w, cos_t, sin_t)


def _softmax_block(s_list, v_list):
    m = s_list[0].max(axis=-1, keepdims=True)
    for s in s_list[1:]:
        m = jnp.maximum(m, s.max(axis=-1, keepdims=True))
    den = None
    o = None
    for s, v in zip(s_list, v_list):
        p = jnp.exp(s - m)
        d = jnp.sum(p, axis=-1, keepdims=True)
        pv = jnp.dot(p.astype(BF16), v.astype(BF16), preferred_element_type=F32)
        den = d if den is None else den + d
        o = pv if o is None else o + pv
    return o / den, m + jnp.log(den)


def _attn_kernel(*refs, tq):
    ins = refs[:15]
    o_ref = refs[15]
    og_refs, lse_refs = refs[16:16 + N_GROUPS], refs[16 + N_GROUPS:16 + 2 * N_GROUPS]
    ti = pl.program_id(2)
    scale = HEAD_DIM ** -0.5
    qi = lax.broadcasted_iota(jnp.int32, (ATT_BLOCK, ATT_BLOCK), 0)
    ki = lax.broadcasted_iota(jnp.int32, (ATT_BLOCK, ATT_BLOCK), 1)
    for g, (win, dil) in enumerate(ATT_GROUPS):
        q_ref, kc_ref, kp_ref, vc_ref, vp_ref = ins[5 * g:5 * g + 5]
        span = ATT_BLOCK * dil
        nblk = tq // span

        def load(ref, start, dil=dil):
            if dil == 1:
                return ref[pl.ds(start, ATT_BLOCK), :]
            return ref[pl.ds(start, ATT_BLOCK, stride=dil), :]

        def body(i, carry, dil=dil, span=span, nblk=nblk, og_ref=og_refs[g], lse_ref=lse_refs[g],
                 q_ref=q_ref, kc_ref=kc_ref, kp_ref=kp_ref, vc_ref=vc_ref, vp_ref=vp_ref, load=load):
            r = i // nblk
            n = i % nblk
            start = r + span * n
            qb = load(q_ref, start).astype(BF16)
            kc = load(kc_ref, start)
            vc = load(vc_ref, start)
            last = r + span * (nblk - 1)
            if nblk == 1:
                kp = load(kp_ref, last)
                vp = load(vp_ref, last)
                has_prev = ti > 0
            else:
                inner = jnp.maximum(start - span, r)
                first = n == 0
                kp = jnp.where(first, load(kp_ref, last), load(kc_ref, inner))
                vp = jnp.where(first, load(vp_ref, last), load(vc_ref, inner))
                has_prev = jnp.logical_or(ti > 0, n > 0)
            s_c = lax.dot_general(qb, kc.astype(BF16), NT_DIMS, preferred_element_type=F32) * scale
            s_p = lax.dot_general(qb, kp.astype(BF16), NT_DIMS, preferred_element_type=F32) * scale
            s_c = jnp.where(ki <= qi, s_c, -jnp.inf)
            s_p = jnp.where(jnp.logical_and(ki >= qi, has_prev), s_p, -jnp.inf)
            o, lse = _softmax_block([s_c, s_p], [vc, vp])
            if dil == 1:
                og_ref[pl.ds(start, ATT_BLOCK), :] = o
                lse_ref[pl.ds(start, ATT_BLOCK), :] = jnp.broadcast_to(lse, o.shape)
            else:
                og_ref[pl.ds(start, ATT_BLOCK, stride=dil), :] = o
                lse_ref[pl.ds(start, ATT_BLOCK, stride=dil), :] = jnp.broadcast_to(lse, o.shape)
            return carry

        lax.fori_loop(0, tq // ATT_BLOCK, body, 0, unroll=ATT_UNROLL)

    l0, l1, l2 = lse_refs[0][...], lse_refs[1][...], lse_refs[2][...]
    m = jnp.maximum(jnp.maximum(l0, l1), l2)
    e0, e1, e2 = jnp.exp(l0 - m), jnp.exp(l1 - m), jnp.exp(l2 - m)
    z = e0 + e1 + e2
    o_ref[...] = (e0 / z) * og_refs[0][...] + (e1 / z) * og_refs[1][...] + (e2 / z) * og_refs[2][...]


def _attn_prompt(qk3, h3):
    b, t, _ = qk3.shape
    tq = ATT_TILE
    assert t % tq == 0
    nh = N_GROUPS * G_HEADS
    voff = (4 * A_WIDTH + 2 * ATT_WIDTH) // HEAD_DIM
    cur = lambda off: pl.BlockSpec((None, tq, HEAD_DIM), lambda bi, ji, ti, off=off: (bi, ti, off + ji))
    prev = lambda off: pl.BlockSpec((None, tq, HEAD_DIM),
                                    lambda bi, ji, ti, off=off: (bi, jnp.maximum(ti - 1, 0), off + ji))
    in_specs, args = [], []
    for g in range(N_GROUPS):
        in_specs += [cur(g * G_HEADS), cur(nh + g * G_HEADS), prev(nh + g * G_HEADS),
                     cur(voff + g * G_HEADS), prev(voff + g * G_HEADS)]
        args += [qk3, qk3, qk3, h3, h3]
    return pl.pallas_call(
        functools.partial(_attn_kernel, tq=tq),
        grid=(b, G_HEADS, t // tq),
        in_specs=in_specs,
        out_specs=pl.BlockSpec((None, tq, HEAD_DIM), lambda bi, ji, ti: (bi, ti, ji)),
        out_shape=jax.ShapeDtypeStruct((b, t, G_HEADS * HEAD_DIM), F32),
        scratch_shapes=[pltpu.VMEM((tq, HEAD_DIM), F32)] * (2 * N_GROUPS),
        compiler_params=_cparams("parallel", "parallel", "arbitrary"),
        name="attn_prompt",
    )(*args)


def _attn_sample_kernel(*refs, tp, t_valid):
    o_ref = refs[12]
    scale = HEAD_DIM ** -0.5
    outs, lses = [], []
    for g, (win, dil) in enumerate(ATT_GROUPS):
        c_ref, q_ref, k_ref, v_ref = refs[4 * g:4 * g + 4]
        n_back = win // dil
        w = c_ref.shape[1]
        qb = q_ref[...].astype(BF16)
        s_b = lax.dot_general(qb, c_ref[0].astype(BF16), NT_DIMS, preferred_element_type=F32) * scale
        s_n = lax.dot_general(qb, k_ref[...].astype(BF16), NT_DIMS, preferred_element_type=F32) * scale
        tq_b = lax.broadcasted_iota(jnp.int32, (tp, w), 0)
        rb = lax.broadcasted_iota(jnp.int32, (tp, w), 1)
        delta = w + tq_b - rb
        ok_b = jnp.logical_and(delta % dil == 0, delta // dil <= n_back)
        tq_n = lax.broadcasted_iota(jnp.int32, (tp, tp), 0)
        tn = lax.broadcasted_iota(jnp.int32, (tp, tp), 1)
        dn = tq_n - tn
        ok_n = (dn >= 0) & (dn % dil == 0) & (dn // dil <= n_back) & (tn < t_valid)
        s_b = jnp.where(ok_b, s_b, -jnp.inf)
        s_n = jnp.where(ok_n, s_n, -jnp.inf)
        o, lse = _softmax_block([s_b, s_n], [c_ref[1], v_ref[...]])
        outs.append(o)
        lses.append(lse)
    m = jnp.maximum(jnp.maximum(lses[0], lses[1]), lses[2])
    es = [jnp.exp(l - m) for l in lses]
    z = es[0] + es[1] + es[2]
    o_ref[...] = (es[0] / z) * outs[0] + (es[1] / z) * outs[1] + (es[2] / z) * outs[2]


def _attn_sample(qk3, h3, caches, t_valid):
    b, tp, _ = qk3.shape
    nh = N_GROUPS * G_HEADS
    voff = (4 * A_WIDTH + 2 * ATT_WIDTH) // HEAD_DIM
    row = lambda off: pl.BlockSpec((None, tp, HEAD_DIM), lambda bi, ji, off=off: (bi, 0, off + ji))
    in_specs, args = [], []
    for g in range(N_GROUPS):
        w = caches[g].shape[3]
        in_specs += [pl.BlockSpec((None, 2, None, w, HEAD_DIM), lambda bi, ji: (bi, 0, ji, 0, 0)),
                     row(g * G_HEADS), row(nh + g * G_HEADS), row(voff + g * G_HEADS)]
        args += [caches[g], qk3, qk3, h3]
    return pl.pallas_call(
        functools.partial(_attn_sample_kernel, tp=tp, t_valid=t_valid),
        grid=(b, G_HEADS),
        in_specs=in_specs,
        out_specs=pl.BlockSpec((None, tp, HEAD_DIM), lambda bi, ji: (bi, 0, ji)),
        out_shape=jax.ShapeDtypeStruct((b, tp, G_HEADS * HEAD_DIM), F32),
        compiler_params=_cparams("parallel", "arbitrary"),
        name="attn_sample",
    )(*args)


def _out_proj_kernel(oa_ref, ob_ref, ga0_ref, ga1_ref, gb0_ref, gb1_ref, x_ref, g1_ref, sc_ref, sh_ref,
                     nw_ref, wa_ref, wb_ref, wo_ref, x1_ref, n2_ref):
    ya = jnp.dot(oa_ref[...].astype(BF16), wa_ref[...], preferred_element_type=F32)
    yb = jnp.dot(ob_ref[...].astype(BF16), wb_ref[...], preferred_element_type=F32)
    gate_a = _sigmoid(jnp.concatenate([ga0_ref[...], ga1_ref[...]], axis=-1))
    gate_b = _sigmoid(jnp.concatenate([gb0_ref[...], gb1_ref[...]], axis=-1))
    mix = jnp.dot((gate_a * ya + gate_b * yb).astype(BF16), wo_ref[...], preferred_element_type=F32)
    x1 = x_ref[...] + g1_ref[...] * mix
    x1_ref[...] = x1
    n2_ref[...] = (_rms(x1) * nw_ref[...]) * (1.0 + sc_ref[...]) + sh_ref[...]


def _out_proj(oa2, ob2, h2, x2, g1, sc2, sh2, nw2, wa_bf, wb_bf, wo_bf, rows_per_group):
    n, d = x2.shape
    tm = min(512, n, rows_per_group if g1.ndim == 3 else n)
    gw = 512
    goff = (4 * A_WIDTH + 3 * ATT_WIDTH) // gw
    rows = lambda width: pl.BlockSpec((tm, width), lambda i: (i, 0))
    gcol = lambda k: pl.BlockSpec((tm, gw), lambda i, k=k: (i, goff + k))
    full = lambda a: pl.BlockSpec(a.shape, lambda i: (0, 0))
    mod = lambda a: _mod_spec(a, rows_per_group, tm, 1)
    return pl.pallas_call(
        _out_proj_kernel,
        grid=(n // tm,),
        in_specs=[rows(A_WIDTH), rows(G_HEADS * HEAD_DIM), gcol(0), gcol(1), gcol(2), gcol(3), rows(d),
                  mod(g1), mod(sc2), mod(sh2), pl.BlockSpec((1, d), lambda i: (0, 0)),
                  full(wa_bf), full(wb_bf), full(wo_bf)],
        out_specs=[rows(d), rows(d)],
        out_shape=[jax.ShapeDtypeStruct((n, d), F32), jax.ShapeDtypeStruct((n, d), F32)],
        compiler_params=_cparams("parallel"),
        name="out_proj",
    )(oa2, ob2, h2, h2, h2, h2, x2, g1, sc2, sh2, nw2.reshape(1, d), wa_bf, wb_bf, wo_bf)


def _topk_rows(s, k, extra=None):
    rows = s.shape[0]
    ridx = lax.broadcasted_iota(jnp.int32, s.shape, 0).astype(F32)
    vals, idxs, extras = [], [], []
    for _ in range(k):
        m = jnp.max(s, axis=0, keepdims=True)
        idx = jnp.min(jnp.where(s == m, ridx, float(rows)), axis=0, keepdims=True)
        hit = ridx == idx
        vals.append(m)
        idxs.append(idx.astype(jnp.int32))
        if extra is not None:
            extras.append(jnp.sum(jnp.where(hit, extra, 0), axis=0, keepdims=True))
        s = jnp.where(hit, -jnp.inf, s)
    return vals, idxs, extras


CAND_CELLS = tuple((a, b) for a in range(P_TOPK) for b in range(P_TOPK) if (a + 1) * (b + 1) <= P_TOPK)
CAND_PAD = -len(CAND_CELLS) % SUBLANES
SEL_TILE = 256


def _peer_sel_kernel(n2_ref, wq_ref, qn_ref, k1_ref, k2_ref, e_ref, g_ref, q_ref, *, tm):
    q = jnp.dot(n2_ref[...].astype(BF16), wq_ref[...], preferred_element_type=F32)
    q_ref[...] = _rms(q) * qn_ref[...]
    hi = lax.Precision.HIGHEST

    tr = min(SEL_TILE, tm)

    def tile(ti, carry):
        q = q_ref[pl.ds(pl.multiple_of(ti * tr, tr), tr), :]
        s1 = lax.dot_general(k1_ref[...], q[:, :P_HALF], NT_DIMS, preferred_element_type=F32, precision=hi)
        s2 = lax.dot_general(k2_ref[...], q[:, P_HALF:], NT_DIMS, preferred_element_type=F32, precision=hi)
        v1, i1, _ = _topk_rows(s1, P_TOPK)
        v2, i2, _ = _topk_rows(s2, P_TOPK)
        cand = [v1[a] + v2[b] for a, b in CAND_CELLS]
        cidx = [i1[a] * N_KEYS + i2[b] for a, b in CAND_CELLS]
        cand += [jnp.full_like(cand[0], -jnp.inf)] * CAND_PAD
        cidx += [jnp.zeros_like(cidx[0])] * CAND_PAD
        sc, _, eidx = _topk_rows(jnp.concatenate(cand, axis=0), P_TOPK, extra=jnp.concatenate(cidx, axis=0))
        sc = jnp.concatenate(sc, axis=0)
        p = jnp.exp(sc - jnp.max(sc, axis=0, keepdims=True))
        cols = pl.ds(pl.multiple_of(ti * tr, tr), tr)
        g_ref[:, cols] = p / jnp.sum(p, axis=0, keepdims=True)
        e_ref[:, cols] = jnp.concatenate(eidx, axis=0)
        return carry

    lax.fori_loop(0, tm // tr, tile, 0)


def _peer_sel(n2, wq_bf, qn_w, k1, k2):
    n, d = n2.shape
    tm = min(256, n)
    assert tm % min(SEL_TILE, tm) == 0
    return pl.pallas_call(
        functools.partial(_peer_sel_kernel, tm=tm),
        grid=(n // tm, P_HEADS),
        in_specs=[pl.BlockSpec((tm, d), lambda i, h: (i, 0)),
                  pl.BlockSpec((d, P_QDIM), lambda i, h: (0, h)),
                  pl.BlockSpec((1, P_QDIM), lambda i, h: (0, 0)),
                  pl.BlockSpec((None, N_KEYS, P_HALF), lambda i, h: (h, 0, 0)),
                  pl.BlockSpec((None, N_KEYS, P_HALF), lambda i, h: (h, 0, 0))],
        out_specs=[pl.BlockSpec((None, P_TOPK, tm), lambda i, h: (h, 0, i)),
                   pl.BlockSpec((None, P_TOPK, tm), lambda i, h: (h, 0, i))],
        out_shape=[jax.ShapeDtypeStruct((P_HEADS, P_TOPK, n), jnp.int32),
                   jax.ShapeDtypeStruct((P_HEADS, P_TOPK, n), F32)],
        scratch_shapes=[pltpu.VMEM((tm, P_QDIM), F32)],
        compiler_params=_cparams("parallel", "arbitrary"),
        name="peer_sel",
    )(n2, wq_bf, qn_w.reshape(1, P_QDIM), k1, k2)


HI_HALF = 0xFFFF0000
ROWS_PER_EXPERT = 4


HALF_SLOTS = P_PAIRS // 2


def _pack_table(tab):
    e, d = tab.shape
    assert d == 2 * ROWS_PER_EXPERT * LANES
    bits = lax.bitcast_convert_type(tab.astype(BF16), jnp.uint16).astype(jnp.uint32)
    bits = bits.reshape(e, 2, ROWS_PER_EXPERT, LANES)
    packed = (bits[:, 0] | (bits[:, 1] << 16)).reshape(e * ROWS_PER_EXPERT, LANES)
    return jnp.pad(packed, ((ROWS_PER_EXPERT, ROWS_PER_EXPERT), (0, 0)))


def _table_rows(e_rm):
    first = (e_rm + 1) * ROWS_PER_EXPERT
    slot = lax.broadcasted_iota(jnp.int32, e_rm.shape, 1)
    return jnp.where(slot < HALF_SLOTS, first, first - ROWS_PER_EXPERT)


def _pair_halves(tab_ref, row_a, row_b, top):
    tile = jnp.where(top, tab_ref[pl.ds(row_a, SUBLANES), :], tab_ref[pl.ds(row_b, SUBLANES), :])
    lo = lax.bitcast_convert_type(tile << jnp.uint32(16), F32)
    hi = lax.bitcast_convert_type(tile & jnp.uint32(HI_HALF), F32)
    return lo, hi


def _top_half():
    return lax.broadcasted_iota(jnp.int32, (SUBLANES, LANES), 0) < ROWS_PER_EXPERT


def _gelu(x):
    return 0.5 * x * (1.0 + jnp.tanh(0.7978845608028654 * (x + 0.044715 * x * x * x)))


def _smem_block_copies(srcs, dsts, sem, tb, step, slot):
    n = tb * P_PAIRS
    return [pltpu.make_async_copy(src.at[pl.ds(step * n, n)], dst.at[pl.ds(pl.multiple_of(slot * n, n), n)],
                                  sem.at[k, slot])
            for k, (src, dst) in enumerate(zip(srcs, dsts))]


def _smem_fetch(srcs, dsts, sem, tb):
    i = pl.program_id(0)
    slot = i % 2

    @pl.when(i == 0)
    def _():
        for cp in _smem_block_copies(srcs, dsts, sem, tb, 0, 0):
            cp.start()

    @pl.when(i + 1 < pl.num_programs(0))
    def _():
        for cp in _smem_block_copies(srcs, dsts, sem, tb, i + 1, 1 - slot):
            cp.start()

    for cp in _smem_block_copies(srcs, dsts, sem, tb, i, slot):
        cp.wait()
    return slot * (tb * P_PAIRS)


TILE_CHUNK = 16
N_CHUNKS = HALF_SLOTS // TILE_CHUNK
U_TILE_CHUNK = 32
U_CHUNKS = HALF_SLOTS // U_TILE_CHUNK
FOLD = ROWS_PER_EXPERT // U_CHUNKS
LANE_ROWS = P_PAIRS // U_CHUNKS


def _stage_sublane(sum_ref, k):
    return sum_ref[pl.ds(k, HALF_SLOTS, stride=SUBLANES), :]


def _diag_rows(col, first_row):
    rows = col.shape[0]
    eye = (lax.broadcasted_iota(jnp.int32, (rows, LANES), 0) + first_row
           == lax.broadcasted_iota(jnp.int32, (rows, LANES), 1))
    return jnp.sum(jnp.where(eye, col, 0.0), axis=0, keepdims=True)


def _peer_u_row(tab_ref, idx_smem, off, x_ref, t, fill_ref, sum_ref, part_ref, t_lane):
    x_row = x_ref[pl.ds(t, 1), :]
    pieces = [x_row[:, s * LANES:(s + 1) * LANES] for s in range(SUBLANES)]
    x_lo = jnp.concatenate(pieces[:ROWS_PER_EXPERT] * 2, axis=0)
    x_hi = jnp.concatenate(pieces[ROWS_PER_EXPERT:] * 2, axis=0)
    top = _top_half()

    def chunk(c, carry):
        first, second, act_row, prev_col = carry
        act_row = act_row + _diag_rows(prev_col, (c - 1) * LANE_ROWS)
        slots = part_ref[t_lane, pl.ds(pl.multiple_of(c * LANE_ROWS, LANE_ROWS), LANE_ROWS), :]
        col = jnp.sum(slots, axis=1, keepdims=True)
        q0 = c * U_TILE_CHUNK
        base = off + t * P_PAIRS + q0
        for p in range(U_TILE_CHUNK):
            lo, hi = _pair_halves(tab_ref, idx_smem[base + p], idx_smem[base + HALF_SLOTS + p], top)
            fill_ref[pl.ds(pl.multiple_of((q0 + p) * SUBLANES, SUBLANES), SUBLANES), :] = lo * x_lo + hi * x_hi
        for k in range(FOLD):
            first = first + _stage_sublane(sum_ref, c * FOLD + k)
            second = second + _stage_sublane(sum_ref, ROWS_PER_EXPERT + c * FOLD + k)
        return first, second, act_row, col

    zero = jnp.zeros((HALF_SLOTS, LANES), F32)
    init = (zero, zero, jnp.zeros((1, LANES), F32), jnp.zeros((LANE_ROWS, 1), F32))
    first, second, act_row, col = lax.fori_loop(0, U_CHUNKS, chunk, init)
    act_row = act_row + _diag_rows(col, (U_CHUNKS - 1) * LANE_ROWS)
    return jnp.concatenate([first, second], axis=0), act_row


def _peer_u_kernel(idx_hbm, x_ref, g_ref, tab_ref, w_ref, idx_smem, sem, stage_a, stage_b, part_ref, act_ref, *, tb):
    off = _smem_fetch((idx_hbm,), (idx_smem,), sem, tb)
    stage_b[...] = jnp.zeros(stage_b.shape, F32)
    part_ref[0] = jnp.zeros(part_ref.shape[1:], F32)

    def row(t, fill_ref, sum_ref):
        done = jnp.maximum(t - 2, 0)
        part, act_row = _peer_u_row(tab_ref, idx_smem, off, x_ref, t, fill_ref, sum_ref, part_ref, done)
        act_ref[pl.ds(done, 1), :] = act_row
        part_ref[jnp.maximum(t - 1, 0)] = part

    def two_rows(j, carry):
        row(2 * j, stage_a, stage_b)
        row(2 * j + 1, stage_b, stage_a)
        return carry

    lax.fori_loop(0, tb // 2, two_rows, 0)
    halves = []
    for h in range(2):
        s = _stage_sublane(stage_b, h * ROWS_PER_EXPERT)
        for k in range(1, ROWS_PER_EXPERT):
            s = s + _stage_sublane(stage_b, h * ROWS_PER_EXPERT + k)
        halves.append(s)
    part_ref[tb - 1] = jnp.concatenate(halves, axis=0)

    for t in (tb - 2, tb - 1):
        act_ref[pl.ds(t, 1), :] = _diag_rows(jnp.sum(part_ref[t], axis=1, keepdims=True), 0)
    w_ref[...] = g_ref[...] * _gelu(act_ref[...])


def _peer_u(idx, x2, g2d, tab):
    n, d = x2.shape
    tb = min(128, n)
    assert tb % 2 == 0 and n % tb == 0
    return pl.pallas_call(
        functools.partial(_peer_u_kernel, tb=tb),
        grid=(n // tb,),
        in_specs=[pl.BlockSpec(memory_space=pl.ANY),
                  pl.BlockSpec((tb, d), lambda i: (i, 0)),
                  pl.BlockSpec((tb, P_PAIRS), lambda i: (i, 0)),
                  pl.BlockSpec(tab.shape, lambda i: (0, 0), pipeline_mode=pl.Buffered(1))],
        out_specs=pl.BlockSpec((tb, P_PAIRS), lambda i: (i, 0)),
        out_shape=jax.ShapeDtypeStruct((n, P_PAIRS), F32),
        scratch_shapes=[pltpu.SMEM((2 * tb * P_PAIRS,), jnp.int32), pltpu.SemaphoreType.DMA((1, 2)),
                        pltpu.VMEM((HALF_SLOTS * SUBLANES, LANES), F32),
                        pltpu.VMEM((HALF_SLOTS * SUBLANES, LANES), F32),
                        pltpu.VMEM((tb, P_PAIRS, LANES), F32),
                        pltpu.VMEM((tb, P_PAIRS), F32)],
        compiler_params=_cparams("arbitrary"),
        name="peer_u",
    )(idx, x2, g2d, tab)


def _peer_v_kernel(idx_hbm, w_hbm, x1_ref, g2_ref, tab_ref, y_ref, idx_smem, w_smem, sem, out_ref, *, tb):
    off = _smem_fetch((idx_hbm, w_hbm), (idx_smem, w_smem), sem, tb)
    nacc = 2
    top = _top_half()

    def tok(t, carry):
        def chunk(c, accs):
            los, his = list(accs[:nacc]), list(accs[nacc:])
            base = off + t * P_PAIRS + c * TILE_CHUNK
            for p in range(TILE_CHUNK):
                a, b = base + p, base + HALF_SLOTS + p
                lo, hi = _pair_halves(tab_ref, idx_smem[a], idx_smem[b], top)
                w = jnp.where(top, w_smem[a], w_smem[b])
                los[p % nacc] = los[p % nacc] + w * lo
                his[p % nacc] = his[p % nacc] + w * hi
            return tuple(los + his)

        zero = jnp.zeros((SUBLANES, LANES), F32)
        accs = lax.fori_loop(0, N_CHUNKS, chunk, (zero,) * (2 * nacc))
        lo, hi = accs[0] + accs[1], accs[2] + accs[3]
        tile = jnp.concatenate([lo[:ROWS_PER_EXPERT] + lo[ROWS_PER_EXPERT:],
                                hi[:ROWS_PER_EXPERT] + hi[ROWS_PER_EXPERT:]], axis=0)
        out_ref[pl.ds(t, 1), :] = jnp.concatenate([tile[s:s + 1, :] for s in range(SUBLANES)], axis=1)
        return carry

    lax.fori_loop(0, tb, tok, 0)
    y_ref[...] = x1_ref[...] + g2_ref[...] * out_ref[...]


def _peer_v(idx, w2d, x1, g2, tab, rows_per_group):
    n, d = x1.shape
    tb = min(256, n, rows_per_group if g2.ndim == 3 else n)
    return pl.pallas_call(
        functools.partial(_peer_v_kernel, tb=tb),
        grid=(n // tb,),
        in_specs=[pl.BlockSpec(memory_space=pl.ANY),
                  pl.BlockSpec(memory_space=pl.ANY),
                  pl.BlockSpec((tb, d), lambda i: (i, 0)),
                  _mod_spec(g2, rows_per_group, tb, 1),
                  pl.BlockSpec(tab.shape, lambda i: (0, 0), pipeline_mode=pl.Buffered(1))],
        out_specs=pl.BlockSpec((tb, d), lambda i: (i, 0)),
        out_shape=jax.ShapeDtypeStruct((n, d), F32),
        scratch_shapes=[pltpu.SMEM((2 * tb * P_PAIRS,), jnp.int32), pltpu.SMEM((2 * tb * P_PAIRS,), F32),
                        pltpu.SemaphoreType.DMA((2, 2)), pltpu.VMEM((tb, d), F32)],
        compiler_params=_cparams("arbitrary"),
        name="peer_v",
    )(idx, w2d, x1, g2, tab)


def _group_mod(m):
    return m[:, None, :]


def _layer(x, mod, pos, lb, s0, kv_bufs, wts):
    (norm1_w, norm2_w, w_in_bf, hgrn_norm_w, qk_w, wa_bf, wb_bf, wo_bf, wq_bf, peer_qn_w,
     peer_k1, peer_k2, u_tab, v_tab) = wts
    b, t, d = x.shape
    n = b * t
    sample = kv_bufs is not None
    sh1, sc1, g1, sh2, sc2, g2 = jnp.split(mod, 6, axis=-1)
    if sample:
        per_row = lambda m: jnp.repeat(m, t, axis=0)
        sh1, sc1, g1, sh2, sc2, g2 = map(per_row, (sh1, sc1, g1, sh2, sc2, g2))
    else:
        sh1, sc1, g1, sh2, sc2, g2 = map(_group_mod, (sh1, sc1, g1, sh2, sc2, g2))
    x2 = x.reshape(n, d)
    h2 = _in_proj(x2, sc1, sh1, norm1_w, w_in_bf, t)
    h3 = h2.reshape(b, t, IN_COLS)

    if sample:
        tp = HGRN_SUB
        h3p = jnp.pad(h3, ((0, 0), (0, tp - t), (0, 0)))
        pos_p = pos[0] + jnp.arange(tp)
        oa, s_new = _hgrn(h3p, lb, s0, hgrn_norm_w, t_valid=t)
        cos_t, sin_t = _rope_tables(pos_p)
        qk3 = _qk_rope(h3p, qk_w, cos_t, sin_t)
        ob = _attn_sample(qk3, h3p, kv_bufs, t)
        oa, ob, qk3 = oa[:, :t], ob[:, :t], qk3[:, :t]
    else:
        oa, s_new = _hgrn(h3, lb, s0, hgrn_norm_w)
        cos_t, sin_t = _rope_tables(pos)
        qk3 = _qk_rope(h3, qk_w, cos_t, sin_t)
        ob = _attn_prompt(qk3, h3)

    voff = 4 * A_WIDTH + 2 * ATT_WIDTH
    gw = G_HEADS * HEAD_DIM
    new_kv = []
    for gi, (win, _) in enumerate(ATT_GROUPS):
        keep = t if sample else min(win, t)
        head_major = lambda a: a.reshape(b, keep, G_HEADS, HEAD_DIM).transpose(0, 2, 1, 3)
        kg = head_major(qk3[:, t - keep:t, ATT_WIDTH + gi * gw:ATT_WIDTH + (gi + 1) * gw])
        vg = head_major(h3[:, t - keep:t, voff + gi * gw:voff + (gi + 1) * gw])
        new_kv.append(jnp.stack([kg, vg], axis=1))

    x1, n2 = _out_proj(oa.reshape(n, A_WIDTH), ob.reshape(n, G_HEADS * HEAD_DIM), h2, x2, g1, sc2, sh2,
                       norm2_w, wa_bf, wb_bf, wo_bf, t)
    e_t, g_t = _peer_sel(n2, wq_bf, peer_qn_w, peer_k1, peer_k2)
    idx = _table_rows(e_t.reshape(P_PAIRS, n).T).reshape(n * P_PAIRS)
    g2d = g_t.reshape(P_PAIRS, n).T
    w = _peer_u(idx, n2, g2d, u_tab)
    y2 = _peer_v(idx, w.reshape(n * P_PAIRS), x1, g2, v_tab, t)
    return y2.reshape(b, t, d), new_kv, s_new


def kernel(x_prompt, x_sample, cache_kv_g1, cache_kv_g2, cache_kv_g3, state_hgrn, c_prompt, c_sample, w_ada, b_ada, norm1_w, norm2_w, w_in, lb_logits, hgrn_norm_w, q_norm_w, k_norm_w, w_branch_a, w_branch_b, w_out, peer_wq, peer_qn_w, peer_k1, peer_k2, peer_u, peer_v):
    depth = w_ada.shape[0]
    bp, tp_len, _ = x_prompt.shape
    bs, ts_len, _ = x_sample.shape
    pos_p = jnp.arange(tp_len)
    pos_s = PAST_LEN + jnp.arange(ts_len)
    lb_all = jnp.cumsum(jax.nn.softmax(lb_logits.astype(F32), axis=0), axis=0)
    caches = (cache_kv_g1, cache_kv_g2, cache_kv_g3)
    c_all = jnp.concatenate([c_prompt, c_sample], axis=0)
    yp, ys = x_prompt, x_sample
    kvp, kvs = ([], [], []), ([], [], [])
    sp_list, ss_list = [], []
    for l in range(depth):
        mod = _ada(c_all, w_ada[l], b_ada[l])
        wts = (norm1_w[l], norm2_w[l], w_in[l].astype(BF16), hgrn_norm_w[l],
               jnp.stack([q_norm_w[l], k_norm_w[l]])[:, None, :],
               w_branch_a[l].astype(BF16), w_branch_b[l].astype(BF16), w_out[l].astype(BF16),
               peer_wq[l].astype(BF16), peer_qn_w[l], peer_k1[l], peer_k2[l],
               _pack_table(peer_u[l]), _pack_table(peer_v[l]))
        lb = lb_all[l].reshape(A_HEADS, A_DK)
        s0p = jnp.zeros((bp, A_HEADS, A_DK, A_DK), F32)
        yp, nkv_p, sp = _layer(yp, mod[:bp], pos_p, lb, s0p, None, wts)
        ys, nkv_s, ss = _layer(ys, mod[bp:], pos_s, lb, state_hgrn[l],
                               tuple(c[l] for c in caches), wts)
        for gi in range(N_GROUPS):
            kvp[gi].append(nkv_p[gi])
            kvs[gi].append(nkv_s[gi])
        sp_list.append(sp)
        ss_list.append(ss)
    return (yp, ys, jnp.stack(kvp[0]), jnp.stack(kvp[1]), jnp.stack(kvp[2]), jnp.stack(sp_list),
            jnp.stack(kvs[0]), jnp.stack(kvs[1]), jnp.stack(kvs[2]), jnp.stack(ss_list))
```

```python
import functools

import jax
import jax.numpy as jnp
from jax import lax
from jax.experimental import pallas as pl
from jax.experimental.pallas import tpu as pltpu

F32 = jnp.float32
BF16 = jnp.bfloat16

D_MODEL = 1024
PAST_LEN = 16384
A_HEADS = 4
A_DK = 128
A_WIDTH = A_HEADS * A_DK
ATT_GROUPS = ((128, 1), (512, 4), (2048, 16))
N_GROUPS = 3
G_HEADS = 4
HEAD_DIM = 128
ATT_WIDTH = N_GROUPS * G_HEADS * HEAD_DIM
ATT_BLOCK = 128
ROT_DIM = HEAD_DIM // 4
ROPE_THETA = 500000.0
IN_COLS = 4 * A_WIDTH + 3 * ATT_WIDTH + 2 * D_MODEL
P_HEADS = 8
N_KEYS = 128
P_QDIM = 256
P_HALF = P_QDIM // 2
P_TOPK = 16
P_PAIRS = P_HEADS * P_TOPK
EPS = 1e-6

LANES = 128
SUBLANES = 8
HGRN_SUB = 16
ATT_TILE = ATT_BLOCK * ATT_GROUPS[-1][1]
ATT_UNROLL = 4
VMEM_LIMIT = 56 * 1024 * 1024

NT_DIMS = (((1,), (1,)), ((), ()))
TN_DIMS = (((0,), (0,)), ((), ()))


def _cparams(*sem):
    return pltpu.CompilerParams(dimension_semantics=sem, vmem_limit_bytes=VMEM_LIMIT)


def _rms(x):
    return x * lax.rsqrt(jnp.mean(x * x, axis=-1, keepdims=True) + EPS)


def _sigmoid(x):
    return 1.0 / (1.0 + jnp.exp(-x))


def _mod_spec(arr, rows_per_group, tm, ngrid):
    if arr.ndim == 3:
        if ngrid == 2:
            return pl.BlockSpec((None, 1, arr.shape[-1]), lambda i, j: (i * tm // rows_per_group, 0, 0))
        return pl.BlockSpec((None, 1, arr.shape[-1]), lambda i: (i * tm // rows_per_group, 0, 0))
    if ngrid == 2:
        return pl.BlockSpec((tm, arr.shape[-1]), lambda i, j: (i, 0))
    return pl.BlockSpec((tm, arr.shape[-1]), lambda i: (i, 0))


def _ada_kernel(c_ref, w_ref, b_ref, o_ref):
    c = c_ref[...]
    s = c * _sigmoid(c)
    o_ref[...] = jnp.dot(s, w_ref[...], preferred_element_type=F32) + b_ref[...]


def _ada(c, w, b):
    rows, d = c.shape
    n = -(-rows // SUBLANES) * SUBLANES
    c = jnp.pad(c, ((0, n - rows), (0, 0)))
    cols = w.shape[1]
    tn = 1024
    out = pl.pallas_call(
        _ada_kernel,
        grid=(cols // tn,),
        in_specs=[pl.BlockSpec((n, d), lambda j: (0, 0)),
                  pl.BlockSpec((d, tn), lambda j: (0, j)),
                  pl.BlockSpec((1, tn), lambda j: (0, j))],
        out_specs=pl.BlockSpec((n, tn), lambda j: (0, j)),
        out_shape=jax.ShapeDtypeStruct((n, cols), F32),
        compiler_params=_cparams("arbitrary"),
        name="ada",
    )(c, w, b.reshape(1, cols))
    return out[:rows]


def _in_proj_kernel(x_ref, sc_ref, sh_ref, nw_ref, w_ref, o_ref, xn_ref):
    @pl.when(pl.program_id(1) == 0)
    def _():
        y = _rms(x_ref[...]) * nw_ref[...]
        y = y * (1.0 + sc_ref[...]) + sh_ref[...]
        xn_ref[...] = y.astype(BF16)

    o_ref[...] = jnp.dot(xn_ref[...], w_ref[...], preferred_element_type=F32)


def _in_proj(x2, sc, sh, nw, w_bf, rows_per_group):
    n, d = x2.shape
    cols = w_bf.shape[1]
    tm = min(1024, n, rows_per_group if sc.ndim == 3 else n)
    tn = cols // 4
    assert tn % LANES == 0
    return pl.pallas_call(
        _in_proj_kernel,
        grid=(n // tm, cols // tn),
        in_specs=[pl.BlockSpec((tm, d), lambda i, j: (i, 0)),
                  _mod_spec(sc, rows_per_group, tm, 2),
                  _mod_spec(sh, rows_per_group, tm, 2),
                  pl.BlockSpec((1, d), lambda i, j: (0, 0)),
                  pl.BlockSpec((d, tn), lambda i, j: (0, j))],
        out_specs=pl.BlockSpec((tm, tn), lambda i, j: (i, j)),
        out_shape=jax.ShapeDtypeStruct((n, cols), F32),
        scratch_shapes=[pltpu.VMEM((tm, d), BF16)],
        compiler_params=_cparams("parallel", "arbitrary"),
        name="in_proj",
    )(x2, sc, sh, nw.reshape(1, d), w_bf)


def _hgrn_kernel(q_ref, f_ref, i_ref, g_ref, lb_ref, s0_ref, nw_ref, o_ref, sout_ref,
                 st_ref, b_ref, k_ref, *, tt, sub, t_valid):
    t = pl.program_id(1)

    @pl.when(t == 0)
    def _():
        for h in range(A_HEADS):
            st_ref[h] = s0_ref[h].T

    lb = lb_ref[...]
    f = lb + (1.0 - lb) * _sigmoid(f_ref[...])
    logf = jnp.log(f)
    kk = 1.0 - f
    if t_valid is not None:
        live = lax.broadcasted_iota(jnp.int32, (tt, 1), 0) < t_valid
        logf = jnp.where(live, logf, 0.0)
        kk = jnp.where(live, kk, 0.0)
    r = lax.broadcasted_iota(jnp.int32, (tt, tt), 0)
    c = lax.broadcasted_iota(jnp.int32, (tt, tt), 1)
    tri = ((r // sub == c // sub) & (c <= r)).astype(F32)
    b_ref[...] = jnp.dot(tri, logf, preferred_element_type=F32, precision=lax.Precision.HIGHEST)
    k_ref[...] = kk

    row = lax.broadcasted_iota(jnp.int32, (sub, 1), 0)
    nw = nw_ref[...]

    def body(i, carry):
        sl = pl.ds(pl.multiple_of(i * sub, sub), sub)
        for h in range(A_HEADS):
            cols = slice(h * A_DK, (h + 1) * A_DK)
            bq = b_ref[sl, cols]
            kq = k_ref[sl, cols]
            qq = q_ref[sl, cols]
            vv = i_ref[sl, cols]
            st = st_ref[h]
            qe = qq * jnp.exp(bq)
            o = lax.dot_general(qe.astype(BF16), st.astype(BF16), NT_DIMS, preferred_element_type=F32)
            for s in range(sub):
                dd = jnp.where(row >= s, bq - bq[s:s + 1, :], -jnp.inf)
                m = qq * kq[s:s + 1, :] * jnp.exp(dd)
                o = o + jnp.sum(m, axis=-1, keepdims=True) * vv[s:s + 1, :]
            bl = bq[sub - 1:sub, :]
            kd = kq * jnp.exp(bl - bq)
            st_ref[h] = st * jnp.exp(bl) + lax.dot_general(vv.astype(BF16), kd.astype(BF16), TN_DIMS,
                                                           preferred_element_type=F32)
            gq = g_ref[sl, cols]
            o_ref[sl, cols] = _rms(o) * nw * (gq * _sigmoid(gq))
        return carry

    lax.fori_loop(0, tt // sub, body, 0)

    @pl.when(t == pl.num_programs(1) - 1)
    def _():
        for h in range(A_HEADS):
            sout_ref[h] = st_ref[h].T


def _hgrn(h3, lb, s0, nw, t_valid=None):
    b, t, _ = h3.shape
    tt = min(256, t)
    sub = min(HGRN_SUB, tt)
    blk = lambda off: pl.BlockSpec((None, tt, A_WIDTH), lambda bi, ti: (bi, ti, off))
    return pl.pallas_call(
        functools.partial(_hgrn_kernel, tt=tt, sub=sub, t_valid=t_valid),
        grid=(b, t // tt),
        in_specs=[blk(0), blk(1), blk(2), blk(3),
                  pl.BlockSpec((1, A_WIDTH), lambda bi, ti: (0, 0)),
                  pl.BlockSpec((None, A_HEADS, A_DK, A_DK), lambda bi, ti: (bi, 0, 0, 0)),
                  pl.BlockSpec((1, A_DK), lambda bi, ti: (0, 0))],
        out_specs=[pl.BlockSpec((None, tt, A_WIDTH), lambda bi, ti: (bi, ti, 0)),
                   pl.BlockSpec((None, A_HEADS, A_DK, A_DK), lambda bi, ti: (bi, 0, 0, 0))],
        out_shape=[jax.ShapeDtypeStruct((b, t, A_WIDTH), F32),
                   jax.ShapeDtypeStruct((b, A_HEADS, A_DK, A_DK), F32)],
        scratch_shapes=[pltpu.VMEM((A_HEADS, A_DK, A_DK), F32), pltpu.VMEM((tt, A_WIDTH), F32),
                        pltpu.VMEM((tt, A_WIDTH), F32)],
        compiler_params=_cparams("parallel", "arbitrary"),
        name="hgrn",
    )(h3, h3, h3, h3, lb.reshape(1, A_WIDTH), s0, nw.reshape(1, A_DK))


ROPE_HEADS = 4


def _qk_rope_kernel(h_ref, w_ref, cos_ref, sin_ref, o_ref):
    half = ROT_DIM // 2
    lane = lax.broadcasted_iota(jnp.int32, cos_ref.shape, 1)
    for k in range(ROPE_HEADS):
        cols = slice(k * HEAD_DIM, (k + 1) * HEAD_DIM)
        y = _rms(h_ref[:, cols]) * w_ref[...]
        partner = jnp.where(lane < half, pltpu.roll(y, LANES - half, axis=1), pltpu.roll(y, half, axis=1))
        o_ref[:, cols] = y * cos_ref[...] + partner * sin_ref[...]


def _rope_tables(pos):
    inv = ROPE_THETA ** (-jnp.arange(0, ROT_DIM, 2, dtype=F32) / ROT_DIM)
    ang = pos.astype(F32)[:, None] * inv[None, :]
    cos, sin = jnp.cos(ang), jnp.sin(ang)
    rest = HEAD_DIM - ROT_DIM
    cos_t = jnp.concatenate([cos, cos, jnp.ones((pos.shape[0], rest), F32)], axis=-1)
    sin_t = jnp.concatenate([-sin, sin, jnp.zeros((pos.shape[0], rest), F32)], axis=-1)
    return cos_t, sin_t


def _qk_rope(h3, qk_w, cos_t, sin_t):
    b, t, _ = h3.shape
    tt = min(1024, t)
    width = ROPE_HEADS * HEAD_DIM
    nq = ATT_WIDTH // width
    off = 4 * A_WIDTH // width
    return pl.pallas_call(
        _qk_rope_kernel,
        grid=(b, t // tt, 2 * nq),
        in_specs=[pl.BlockSpec((None, tt, width), lambda bi, ti, hi: (bi, ti, off + hi)),
                  pl.BlockSpec((None, 1, HEAD_DIM), lambda bi, ti, hi: (hi // nq, 0, 0)),
                  pl.BlockSpec((tt, HEAD_DIM), lambda bi, ti, hi: (ti, 0)),
                  pl.BlockSpec((tt, HEAD_DIM), lambda bi, ti, hi: (ti, 0))],
        out_specs=pl.BlockSpec((None, tt, width), lambda bi, ti, hi: (bi, ti, hi)),
        out_shape=jax.ShapeDtypeStruct((b, t, 2 * ATT_WIDTH), F32),
        compiler_params=_cparams("parallel", "parallel", "arbitrary"),
        name="qk_rope",
    )(h3, qk_w, cos_t, sin_t)


def _softmax_block(s_list, v_list):
    m = s_list[0].max(axis=-1, keepdims=True)
    for s in s_list[1:]:
        m = jnp.maximum(m, s.max(axis=-1, keepdims=True))
    den = None
    o = None
    for s, v in zip(s_list, v_list):
        p = jnp.exp(s - m)
        d = jnp.sum(p, axis=-1, keepdims=True)
        pv = jnp.dot(p.astype(BF16), v.astype(BF16), preferred_element_type=F32)
        den = d if den is None else den + d
        o = pv if o is None else o + pv
    return o / den, m + jnp.log(den)


def _attn_kernel(*refs, tq):
    ins = refs[:15]
    o_ref = refs[15]
    og_refs, lse_refs = refs[16:16 + N_GROUPS], refs[16 + N_GROUPS:16 + 2 * N_GROUPS]
    ti = pl.program_id(2)
    scale = HEAD_DIM ** -0.5
    qi = lax.broadcasted_iota(jnp.int32, (ATT_BLOCK, ATT_BLOCK), 0)
    ki = lax.broadcasted_iota(jnp.int32, (ATT_BLOCK, ATT_BLOCK), 1)
    for g, (win, dil) in enumerate(ATT_GROUPS):
        q_ref, kc_ref, kp_ref, vc_ref, vp_ref = ins[5 * g:5 * g + 5]
        span = ATT_BLOCK * dil
        nblk = tq // span

        def load(ref, start, dil=dil):
            if dil == 1:
                return ref[pl.ds(start, ATT_BLOCK), :]
            return ref[pl.ds(start, ATT_BLOCK, stride=dil), :]

        def body(i, carry, dil=dil, span=span, nblk=nblk, og_ref=og_refs[g], lse_ref=lse_refs[g],
                 q_ref=q_ref, kc_ref=kc_ref, kp_ref=kp_ref, vc_ref=vc_ref, vp_ref=vp_ref, load=load):
            r = i // nblk
            n = i % nblk
            start = r + span * n
            qb = load(q_ref, start).astype(BF16)
            kc = load(kc_ref, start)
            vc = load(vc_ref, start)
            last = r + span * (nblk - 1)
            if nblk == 1:
                kp = load(kp_ref, last)
                vp = load(vp_ref, last)
                has_prev = ti > 0
            else:
                inner = jnp.maximum(start - span, r)
                first = n == 0
                kp = jnp.where(first, load(kp_ref, last), load(kc_ref, inner))
                vp = jnp.where(first, load(vp_ref, last), load(vc_ref, inner))
                has_prev = jnp.logical_or(ti > 0, n > 0)
            s_c = lax.dot_general(qb, kc.astype(BF16), NT_DIMS, preferred_element_type=F32) * scale
            s_p = lax.dot_general(qb, kp.astype(BF16), NT_DIMS, preferred_element_type=F32) * scale
            s_c = jnp.where(ki <= qi, s_c, -jnp.inf)
            s_p = jnp.where(jnp.logical_and(ki >= qi, has_prev), s_p, -jnp.inf)
            o, lse = _softmax_block([s_c, s_p], [vc, vp])
            if dil == 1:
                og_ref[pl.ds(start, ATT_BLOCK), :] = o
                lse_ref[pl.ds(start, ATT_BLOCK), :] = jnp.broadcast_to(lse, o.shape)
            else:
                og_ref[pl.ds(start, ATT_BLOCK, stride=dil), :] = o
                lse_ref[pl.ds(start, ATT_BLOCK, stride=dil), :] = jnp.broadcast_to(lse, o.shape)
            return carry

        lax.fori_loop(0, tq // ATT_BLOCK, body, 0, unroll=ATT_UNROLL)

    l0, l1, l2 = lse_refs[0][...], lse_refs[1][...], lse_refs[2][...]
    m = jnp.maximum(jnp.maximum(l0, l1), l2)
    e0, e1, e2 = jnp.exp(l0 - m), jnp.exp(l1 - m), jnp.exp(l2 - m)
    z = e0 + e1 + e2
    o_ref[...] = (e0 / z) * og_refs[0][...] + (e1 / z) * og_refs[1][...] + (e2 / z) * og_refs[2][...]


def _attn_prompt(qk3, h3):
    b, t, _ = qk3.shape
    tq = ATT_TILE
    assert t % tq == 0
    nh = N_GROUPS * G_HEADS
    voff = (4 * A_WIDTH + 2 * ATT_WIDTH) // HEAD_DIM
    cur = lambda off: pl.BlockSpec((None, tq, HEAD_DIM), lambda bi, ji, ti, off=off: (bi, ti, off + ji))
    prev = lambda off: pl.BlockSpec((None, tq, HEAD_DIM),
                                    lambda bi, ji, ti, off=off: (bi, jnp.maximum(ti - 1, 0), off + ji))
    in_specs, args = [], []
    for g in range(N_GROUPS):
        in_specs += [cur(g * G_HEADS), cur(nh + g * G_HEADS), prev(nh + g * G_HEADS),
                     cur(voff + g * G_HEADS), prev(voff + g * G_HEADS)]
        args += [qk3, qk3, qk3, h3, h3]
    return pl.pallas_call(
        functools.partial(_attn_kernel, tq=tq),
        grid=(b, G_HEADS, t // tq),
        in_specs=in_specs,
        out_specs=pl.BlockSpec((None, tq, HEAD_DIM), lambda bi, ji, ti: (bi, ti, ji)),
        out_shape=jax.ShapeDtypeStruct((b, t, G_HEADS * HEAD_DIM), F32),
        scratch_shapes=[pltpu.VMEM((tq, HEAD_DIM), F32)] * (2 * N_GROUPS),
        compiler_params=_cparams("parallel", "parallel", "arbitrary"),
        name="attn_prompt",
    )(*args)


def _attn_sample_kernel(*refs, tp, t_valid):
    o_ref = refs[12]
    scale = HEAD_DIM ** -0.5
    outs, lses = [], []
    for g, (win, dil) in enumerate(ATT_GROUPS):
        c_ref, q_ref, k_ref, v_ref = refs[4 * g:4 * g + 4]
        n_back = win // dil
        w = c_ref.shape[1]
        qb = q_ref[...].astype(BF16)
        s_b = lax.dot_general(qb, c_ref[0].astype(BF16), NT_DIMS, preferred_element_type=F32) * scale
        s_n = lax.dot_general(qb, k_ref[...].astype(BF16), NT_DIMS, preferred_element_type=F32) * scale
        tq_b = lax.broadcasted_iota(jnp.int32, (tp, w), 0)
        rb = lax.broadcasted_iota(jnp.int32, (tp, w), 1)
        delta = w + tq_b - rb
        ok_b = jnp.logical_and(delta % dil == 0, delta // dil <= n_back)
        tq_n = lax.broadcasted_iota(jnp.int32, (tp, tp), 0)
        tn = lax.broadcasted_iota(jnp.int32, (tp, tp), 1)
        dn = tq_n - tn
        ok_n = (dn >= 0) & (dn % dil == 0) & (dn // dil <= n_back) & (tn < t_valid)
        s_b = jnp.where(ok_b, s_b, -jnp.inf)
        s_n = jnp.where(ok_n, s_n, -jnp.inf)
        o, lse = _softmax_block([s_b, s_n], [c_ref[1], v_ref[...]])
        outs.append(o)
        lses.append(lse)
    m = jnp.maximum(jnp.maximum(lses[0], lses[1]), lses[2])
    es = [jnp.exp(l - m) for l in lses]
    z = es[0] + es[1] + es[2]
    o_ref[...] = (es[0] / z) * outs[0] + (es[1] / z) * outs[1] + (es[2] / z) * outs[2]


def _attn_sample(qk3, h3, caches, t_valid):
    b, tp, _ = qk3.shape
    nh = N_GROUPS * G_HEADS
    voff = (4 * A_WIDTH + 2 * ATT_WIDTH) // HEAD_DIM
    row = lambda off: pl.BlockSpec((None, tp, HEAD_DIM), lambda bi, ji, off=off: (bi, 0, off + ji))
    in_specs, args = [], []
    for g in range(N_GROUPS):
        w = caches[g].shape[3]
        in_specs += [pl.BlockSpec((None, 2, None, w, HEAD_DIM), lambda bi, ji: (bi, 0, ji, 0, 0)),
                     row(g * G_HEADS), row(nh + g * G_HEADS), row(voff + g * G_HEADS)]
        args += [caches[g], qk3, qk3, h3]
    return pl.pallas_call(
        functools.partial(_attn_sample_kernel, tp=tp, t_valid=t_valid),
        grid=(b, G_HEADS),
        in_specs=in_specs,
        out_specs=pl.BlockSpec((None, tp, HEAD_DIM), lambda bi, ji: (bi, 0, ji)),
        out_shape=jax.ShapeDtypeStruct((b, tp, G_HEADS * HEAD_DIM), F32),
        compiler_params=_cparams("parallel", "arbitrary"),
        name="attn_sample",
    )(*args)


def _out_proj_kernel(oa_ref, ob_ref, ga0_ref, ga1_ref, gb0_ref, gb1_ref, x_ref, g1_ref, sc_ref, sh_ref,
                     nw_ref, wa_ref, wb_ref, wo_ref, x1_ref, n2_ref):
    ya = jnp.dot(oa_ref[...].astype(BF16), wa_ref[...], preferred_element_type=F32)
    yb = jnp.dot(ob_ref[...].astype(BF16), wb_ref[...], preferred_element_type=F32)
    gate_a = _sigmoid(jnp.concatenate([ga0_ref[...], ga1_ref[...]], axis=-1))
    gate_b = _sigmoid(jnp.concatenate([gb0_ref[...], gb1_ref[...]], axis=-1))
    mix = jnp.dot((gate_a * ya + gate_b * yb).astype(BF16), wo_ref[...], preferred_element_type=F32)
    x1 = x_ref[...] + g1_ref[...] * mix
    x1_ref[...] = x1
    n2_ref[...] = (_rms(x1) * nw_ref[...]) * (1.0 + sc_ref[...]) + sh_ref[...]


def _out_proj(oa2, ob2, h2, x2, g1, sc2, sh2, nw2, wa_bf, wb_bf, wo_bf, rows_per_group):
    n, d = x2.shape
    tm = min(512, n, rows_per_group if g1.ndim == 3 else n)
    gw = 512
    goff = (4 * A_WIDTH + 3 * ATT_WIDTH) // gw
    rows = lambda width: pl.BlockSpec((tm, width), lambda i: (i, 0))
    gcol = lambda k: pl.BlockSpec((tm, gw), lambda i, k=k: (i, goff + k))
    full = lambda a: pl.BlockSpec(a.shape, lambda i: (0, 0))
    mod = lambda a: _mod_spec(a, rows_per_group, tm, 1)
    return pl.pallas_call(
        _out_proj_kernel,
        grid=(n // tm,),
        in_specs=[rows(A_WIDTH), rows(G_HEADS * HEAD_DIM), gcol(0), gcol(1), gcol(2), gcol(3), rows(d),
                  mod(g1), mod(sc2), mod(sh2), pl.BlockSpec((1, d), lambda i: (0, 0)),
                  full(wa_bf), full(wb_bf), full(wo_bf)],
        out_specs=[rows(d), rows(d)],
        out_shape=[jax.ShapeDtypeStruct((n, d), F32), jax.ShapeDtypeStruct((n, d), F32)],
        compiler_params=_cparams("parallel"),
        name="out_proj",
    )(oa2, ob2, h2, h2, h2, h2, x2, g1, sc2, sh2, nw2.reshape(1, d), wa_bf, wb_bf, wo_bf)


def _topk_rows(s, k, extra=None):
    rows = s.shape[0]
    ridx = lax.broadcasted_iota(jnp.int32, s.shape, 0).astype(F32)
    vals, idxs, extras = [], [], []
    for _ in range(k):
        m = jnp.max(s, axis=0, keepdims=True)
        idx = jnp.min(jnp.where(s == m, ridx, float(rows)), axis=0, keepdims=True)
        hit = ridx == idx
        vals.append(m)
        idxs.append(idx.astype(jnp.int32))
        if extra is not None:
            extras.append(jnp.sum(jnp.where(hit, extra, 0), axis=0, keepdims=True))
        s = jnp.where(hit, -jnp.inf, s)
    return vals, idxs, extras


CAND_CELLS = tuple((a, b) for a in range(P_TOPK) for b in range(P_TOPK) if (a + 1) * (b + 1) <= P_TOPK)
CAND_PAD = -len(CAND_CELLS) % SUBLANES
SEL_TILE = 256


def _peer_sel_kernel(n2_ref, wq_ref, qn_ref, k1_ref, k2_ref, e_ref, g_ref, q_ref, *, tm):
    q = jnp.dot(n2_ref[...].astype(BF16), wq_ref[...], preferred_element_type=F32)
    q_ref[...] = _rms(q) * qn_ref[...]
    hi = lax.Precision.HIGHEST

    tr = min(SEL_TILE, tm)

    def tile(ti, carry):
        q = q_ref[pl.ds(pl.multiple_of(ti * tr, tr), tr), :]
        s1 = lax.dot_general(k1_ref[...], q[:, :P_HALF], NT_DIMS, preferred_element_type=F32, precision=hi)
        s2 = lax.dot_general(k2_ref[...], q[:, P_HALF:], NT_DIMS, preferred_element_type=F32, precision=hi)
        v1, i1, _ = _topk_rows(s1, P_TOPK)
        v2, i2, _ = _topk_rows(s2, P_TOPK)
        cand = [v1[a] + v2[b] for a, b in CAND_CELLS]
        cidx = [i1[a] * N_KEYS + i2[b] for a, b in CAND_CELLS]
        cand += [jnp.full_like(cand[0], -jnp.inf)] * CAND_PAD
        cidx += [jnp.zeros_like(cidx[0])] * CAND_PAD
        sc, _, eidx = _topk_rows(jnp.concatenate(cand, axis=0), P_TOPK, extra=jnp.concatenate(cidx, axis=0))
        sc = jnp.concatenate(sc, axis=0)
        p = jnp.exp(sc - jnp.max(sc, axis=0, keepdims=True))
        cols = pl.ds(pl.multiple_of(ti * tr, tr), tr)
        g_ref[:, cols] = p / jnp.sum(p, axis=0, keepdims=True)
        e_ref[:, cols] = jnp.concatenate(eidx, axis=0)
        return carry

    lax.fori_loop(0, tm // tr, tile, 0)


def _peer_sel(n2, wq_bf, qn_w, k1, k2):
    n, d = n2.shape
    tm = min(256, n)
    assert tm % min(SEL_TILE, tm) == 0
    return pl.pallas_call(
        functools.partial(_peer_sel_kernel, tm=tm),
        grid=(n // tm, P_HEADS),
        in_specs=[pl.BlockSpec((tm, d), lambda i, h: (i, 0)),
                  pl.BlockSpec((d, P_QDIM), lambda i, h: (0, h)),
                  pl.BlockSpec((1, P_QDIM), lambda i, h: (0, 0)),
                  pl.BlockSpec((None, N_KEYS, P_HALF), lambda i, h: (h, 0, 0)),
                  pl.BlockSpec((None, N_KEYS, P_HALF), lambda i, h: (h, 0, 0))],
        out_specs=[pl.BlockSpec((None, P_TOPK, tm), lambda i, h: (h, 0, i)),
                   pl.BlockSpec((None, P_TOPK, tm), lambda i, h: (h, 0, i))],
        out_shape=[jax.ShapeDtypeStruct((P_HEADS, P_TOPK, n), jnp.int32),
                   jax.ShapeDtypeStruct((P_HEADS, P_TOPK, n), F32)],
        scratch_shapes=[pltpu.VMEM((tm, P_QDIM), F32)],
        compiler_params=_cparams("parallel", "arbitrary"),
        name="peer_sel",
    )(n2, wq_bf, qn_w.reshape(1, P_QDIM), k1, k2)


HI_HALF = 0xFFFF0000
ROWS_PER_EXPERT = 4


HALF_SLOTS = P_PAIRS // 2


def _pack_table(tab):
    e, d = tab.shape
    assert d == 2 * ROWS_PER_EXPERT * LANES
    bits = lax.bitcast_convert_type(tab.astype(BF16), jnp.uint16).astype(jnp.uint32)
    bits = bits.reshape(e, 2, ROWS_PER_EXPERT, LANES)
    packed = (bits[:, 0] | (bits[:, 1] << 16)).reshape(e * ROWS_PER_EXPERT, LANES)
    return jnp.pad(packed, ((ROWS_PER_EXPERT, ROWS_PER_EXPERT), (0, 0)))


def _table_rows(e_rm):
    first = (e_rm + 1) * ROWS_PER_EXPERT
    slot = lax.broadcasted_iota(jnp.int32, e_rm.shape, 1)
    return jnp.where(slot < HALF_SLOTS, first, first - ROWS_PER_EXPERT)


def _pair_halves(tab_ref, row_a, row_b, top):
    tile = jnp.where(top, tab_ref[pl.ds(row_a, SUBLANES), :], tab_ref[pl.ds(row_b, SUBLANES), :])
    lo = lax.bitcast_convert_type(tile << jnp.uint32(16), F32)
    hi = lax.bitcast_convert_type(tile & jnp.uint32(HI_HALF), F32)
    return lo, hi


def _top_half():
    return lax.broadcasted_iota(jnp.int32, (SUBLANES, LANES), 0) < ROWS_PER_EXPERT


def _gelu(x):
    return 0.5 * x * (1.0 + jnp.tanh(0.7978845608028654 * (x + 0.044715 * x * x * x)))


def _by_position(a, tb, chunk):
    n, slots = a.shape
    halves = slots // HALF_SLOTS
    nch = HALF_SLOTS // chunk
    a = a.reshape(n // tb, tb, halves, nch, chunk).transpose(0, 2, 4, 1, 3)
    return a.reshape(n // tb, halves * chunk, tb * nch)


def _pack_weight_pairs(w):
    bits = lax.bitcast_convert_type(w.astype(BF16), jnp.uint16).astype(jnp.uint32)
    packed = bits[:, :HALF_SLOTS] | (bits[:, HALF_SLOTS:] << 16)
    return lax.bitcast_convert_type(packed, jnp.int32)


def _position_copies(srcs, dsts, sem, length, step, slot):
    return [pltpu.make_async_copy(src.at[step, q], dst[q].at[pl.ds(pl.multiple_of(slot * length, length), length)],
                                  sem.at[k, slot])
            for k, (src, dst) in enumerate(zip(srcs, dsts)) for q in range(len(dst))]


def _smem_fetch(srcs, dsts, sem, length):
    i = pl.program_id(0)
    slot = i % 2

    @pl.when(i == 0)
    def _():
        for cp in _position_copies(srcs, dsts, sem, length, 0, 0):
            cp.start()

    @pl.when(i + 1 < pl.num_programs(0))
    def _():
        for cp in _position_copies(srcs, dsts, sem, length, i + 1, 1 - slot):
            cp.start()

    for cp in _position_copies(srcs, dsts, sem, length, i, slot):
        cp.wait()
    return slot * length


TILE_CHUNK = 16
N_CHUNKS = HALF_SLOTS // TILE_CHUNK
U_TILE_CHUNK = 32
U_CHUNKS = HALF_SLOTS // U_TILE_CHUNK
FOLD = ROWS_PER_EXPERT // U_CHUNKS
LANE_ROWS = P_PAIRS // U_CHUNKS


def _stage_sublane(sum_ref, k):
    return sum_ref[pl.ds(k, HALF_SLOTS, stride=SUBLANES), :]


def _diag_rows(col, first_row):
    rows = col.shape[0]
    eye = (lax.broadcasted_iota(jnp.int32, (rows, LANES), 0) + first_row
           == lax.broadcasted_iota(jnp.int32, (rows, LANES), 1))
    return jnp.sum(jnp.where(eye, col, 0.0), axis=0, keepdims=True)


def _peer_u_row(tab_ref, idx_smem, off, x_ref, t, fill_ref, sum_ref, part_ref, t_lane):
    x_row = x_ref[pl.ds(t, 1), :]
    pieces = [x_row[:, s * LANES:(s + 1) * LANES] for s in range(SUBLANES)]
    x_lo = jnp.concatenate(pieces[:ROWS_PER_EXPERT] * 2, axis=0)
    x_hi = jnp.concatenate(pieces[ROWS_PER_EXPERT:] * 2, axis=0)
    top = _top_half()

    def chunk(c, carry):
        first, second, act_row, prev_col = carry
        act_row = act_row + _diag_rows(prev_col, (c - 1) * LANE_ROWS)
        slots = part_ref[t_lane, pl.ds(pl.multiple_of(c * LANE_ROWS, LANE_ROWS), LANE_ROWS), :]
        col = jnp.sum(slots, axis=1, keepdims=True)
        q0 = c * U_TILE_CHUNK
        entry = off + t * U_CHUNKS + c
        for p in range(U_TILE_CHUNK):
            lo, hi = _pair_halves(tab_ref, idx_smem[p][entry], idx_smem[U_TILE_CHUNK + p][entry], top)
            fill_ref[pl.ds(pl.multiple_of((q0 + p) * SUBLANES, SUBLANES), SUBLANES), :] = lo * x_lo + hi * x_hi
        for k in range(FOLD):
            first = first + _stage_sublane(sum_ref, c * FOLD + k)
            second = second + _stage_sublane(sum_ref, ROWS_PER_EXPERT + c * FOLD + k)
        return first, second, act_row, col

    zero = jnp.zeros((HALF_SLOTS, LANES), F32)
    init = (zero, zero, jnp.zeros((1, LANES), F32), jnp.zeros((LANE_ROWS, 1), F32))
    first, second, act_row, col = lax.fori_loop(0, U_CHUNKS, chunk, init)
    act_row = act_row + _diag_rows(col, (U_CHUNKS - 1) * LANE_ROWS)
    return jnp.concatenate([first, second], axis=0), act_row


def _peer_u_kernel(idx_hbm, x_ref, g_ref, tab_ref, w_ref, *scratch, tb):
    nq = 2 * U_TILE_CHUNK
    idx_smem = scratch[:nq]
    sem, stage_a, stage_b, part_ref, act_ref = scratch[nq:]
    off = _smem_fetch((idx_hbm,), (idx_smem,), sem, tb * U_CHUNKS)
    stage_b[...] = jnp.zeros(stage_b.shape, F32)
    part_ref[0] = jnp.zeros(part_ref.shape[1:], F32)

    def row(t, fill_ref, sum_ref):
        done = jnp.maximum(t - 2, 0)
        part, act_row = _peer_u_row(tab_ref, idx_smem, off, x_ref, t, fill_ref, sum_ref, part_ref, done)
        act_ref[pl.ds(done, 1), :] = act_row
        part_ref[jnp.maximum(t - 1, 0)] = part

    def two_rows(j, carry):
        row(2 * j, stage_a, stage_b)
        row(2 * j + 1, stage_b, stage_a)
        return carry

    lax.fori_loop(0, tb // 2, two_rows, 0)
    halves = []
    for h in range(2):
        s = _stage_sublane(stage_b, h * ROWS_PER_EXPERT)
        for k in range(1, ROWS_PER_EXPERT):
            s = s + _stage_sublane(stage_b, h * ROWS_PER_EXPERT + k)
        halves.append(s)
    part_ref[tb - 1] = jnp.concatenate(halves, axis=0)

    for t in (tb - 2, tb - 1):
        act_ref[pl.ds(t, 1), :] = _diag_rows(jnp.sum(part_ref[t], axis=1, keepdims=True), 0)
    w_ref[...] = g_ref[...] * _gelu(act_ref[...])


def _peer_u(rows, x2, g2d, tab):
    n, d = x2.shape
    tb = min(128, n)
    assert tb % 2 == 0 and n % tb == 0
    idx = _by_position(rows, tb, U_TILE_CHUNK)
    return pl.pallas_call(
        functools.partial(_peer_u_kernel, tb=tb),
        grid=(n // tb,),
        in_specs=[pl.BlockSpec(memory_space=pl.ANY),
                  pl.BlockSpec((tb, d), lambda i: (i, 0)),
                  pl.BlockSpec((tb, P_PAIRS), lambda i: (i, 0)),
                  pl.BlockSpec(tab.shape, lambda i: (0, 0), pipeline_mode=pl.Buffered(1))],
        out_specs=pl.BlockSpec((tb, P_PAIRS), lambda i: (i, 0)),
        out_shape=jax.ShapeDtypeStruct((n, P_PAIRS), F32),
        scratch_shapes=[pltpu.SMEM((2 * tb * U_CHUNKS,), jnp.int32)] * (2 * U_TILE_CHUNK) + [
                        pltpu.SemaphoreType.DMA((1, 2)),
                        pltpu.VMEM((HALF_SLOTS * SUBLANES, LANES), F32),
                        pltpu.VMEM((HALF_SLOTS * SUBLANES, LANES), F32),
                        pltpu.VMEM((tb, P_PAIRS, LANES), F32),
                        pltpu.VMEM((tb, P_PAIRS), F32)],
        compiler_params=_cparams("arbitrary"),
        name="peer_u",
    )(idx, x2, g2d, tab)


def _peer_v_kernel(idx_hbm, w_hbm, x1_ref, g2_ref, tab_ref, y_ref, *scratch, tb):
    nq = 2 * TILE_CHUNK
    idx_smem, w_smem = scratch[:nq], scratch[nq:nq + TILE_CHUNK]
    sem, out_ref = scratch[nq + TILE_CHUNK:]
    off = _smem_fetch((idx_hbm, w_hbm), (idx_smem, w_smem), sem, tb * N_CHUNKS)
    nacc = 2
    top = _top_half()

    def tok(t, carry):
        def chunk(c, accs):
            los, his = list(accs[:nacc]), list(accs[nacc:])
            entry = off + t * N_CHUNKS + c
            for p in range(TILE_CHUNK):
                a, b = p, TILE_CHUNK + p
                lo, hi = _pair_halves(tab_ref, idx_smem[a][entry], idx_smem[b][entry], top)
                wv = jnp.full((SUBLANES, LANES), w_smem[p][entry], jnp.int32).astype(jnp.uint32)
                w = lax.bitcast_convert_type(jnp.where(top, wv << jnp.uint32(16), wv & jnp.uint32(HI_HALF)), F32)
                los[p % nacc] = los[p % nacc] + w * lo
                his[p % nacc] = his[p % nacc] + w * hi
            return tuple(los + his)

        zero = jnp.zeros((SUBLANES, LANES), F32)
        accs = lax.fori_loop(0, N_CHUNKS, chunk, (zero,) * (2 * nacc))
        lo, hi = accs[0] + accs[1], accs[2] + accs[3]
        tile = jnp.concatenate([lo[:ROWS_PER_EXPERT] + lo[ROWS_PER_EXPERT:],
                                hi[:ROWS_PER_EXPERT] + hi[ROWS_PER_EXPERT:]], axis=0)
        out_ref[pl.ds(t, 1), :] = jnp.concatenate([tile[s:s + 1, :] for s in range(SUBLANES)], axis=1)
        return carry

    lax.fori_loop(0, tb, tok, 0)
    y_ref[...] = x1_ref[...] + g2_ref[...] * out_ref[...]


def _peer_v(rows, w2d, x1, g2, tab, rows_per_group):
    n, d = x1.shape
    tb = min(256, n, rows_per_group if g2.ndim == 3 else n)
    idx = _by_position(rows, tb, TILE_CHUNK)
    w2d = _by_position(_pack_weight_pairs(w2d), tb, TILE_CHUNK)
    return pl.pallas_call(
        functools.partial(_peer_v_kernel, tb=tb),
        grid=(n // tb,),
        in_specs=[pl.BlockSpec(memory_space=pl.ANY),
                  pl.BlockSpec(memory_space=pl.ANY),
                  pl.BlockSpec((tb, d), lambda i: (i, 0)),
                  _mod_spec(g2, rows_per_group, tb, 1),
                  pl.BlockSpec(tab.shape, lambda i: (0, 0), pipeline_mode=pl.Buffered(1))],
        out_specs=pl.BlockSpec((tb, d), lambda i: (i, 0)),
        out_shape=jax.ShapeDtypeStruct((n, d), F32),
        scratch_shapes=([pltpu.SMEM((2 * tb * N_CHUNKS,), jnp.int32)] * (2 * TILE_CHUNK)
                        + [pltpu.SMEM((2 * tb * N_CHUNKS,), jnp.int32)] * TILE_CHUNK
                        + [pltpu.SemaphoreType.DMA((2, 2)), pltpu.VMEM((tb, d), F32)]),
        compiler_params=_cparams("arbitrary"),
        name="peer_v",
    )(idx, w2d, x1, g2, tab)


def _group_mod(m):
    return m[:, None, :]


def _layer(x, mod, pos, lb, s0, kv_bufs, wts):
    (norm1_w, norm2_w, w_in_bf, hgrn_norm_w, qk_w, wa_bf, wb_bf, wo_bf, wq_bf, peer_qn_w,
     peer_k1, peer_k2, u_tab, v_tab) = wts
    b, t, d = x.shape
    n = b * t
    sample = kv_bufs is not None
    sh1, sc1, g1, sh2, sc2, g2 = jnp.split(mod, 6, axis=-1)
    if sample:
        per_row = lambda m: jnp.repeat(m, t, axis=0)
        sh1, sc1, g1, sh2, sc2, g2 = map(per_row, (sh1, sc1, g1, sh2, sc2, g2))
    else:
        sh1, sc1, g1, sh2, sc2, g2 = map(_group_mod, (sh1, sc1, g1, sh2, sc2, g2))
    x2 = x.reshape(n, d)
    h2 = _in_proj(x2, sc1, sh1, norm1_w, w_in_bf, t)
    h3 = h2.reshape(b, t, IN_COLS)

    if sample:
        tp = HGRN_SUB
        h3p = jnp.pad(h3, ((0, 0), (0, tp - t), (0, 0)))
        pos_p = pos[0] + jnp.arange(tp)
        oa, s_new = _hgrn(h3p, lb, s0, hgrn_norm_w, t_valid=t)
        cos_t, sin_t = _rope_tables(pos_p)
        qk3 = _qk_rope(h3p, qk_w, cos_t, sin_t)
        ob = _attn_sample(qk3, h3p, kv_bufs, t)
        oa, ob, qk3 = oa[:, :t], ob[:, :t], qk3[:, :t]
    else:
        oa, s_new = _hgrn(h3, lb, s0, hgrn_norm_w)
        cos_t, sin_t = _rope_tables(pos)
        qk3 = _qk_rope(h3, qk_w, cos_t, sin_t)
        ob = _attn_prompt(qk3, h3)

    voff = 4 * A_WIDTH + 2 * ATT_WIDTH
    gw = G_HEADS * HEAD_DIM
    new_kv = []
    for gi, (win, _) in enumerate(ATT_GROUPS):
        keep = t if sample else min(win, t)
        head_major = lambda a: a.reshape(b, keep, G_HEADS, HEAD_DIM).transpose(0, 2, 1, 3)
        kg = head_major(qk3[:, t - keep:t, ATT_WIDTH + gi * gw:ATT_WIDTH + (gi + 1) * gw])
        vg = head_major(h3[:, t - keep:t, voff + gi * gw:voff + (gi + 1) * gw])
        new_kv.append(jnp.stack([kg, vg], axis=1))

    x1, n2 = _out_proj(oa.reshape(n, A_WIDTH), ob.reshape(n, G_HEADS * HEAD_DIM), h2, x2, g1, sc2, sh2,
                       norm2_w, wa_bf, wb_bf, wo_bf, t)
    e_t, g_t = _peer_sel(n2, wq_bf, peer_qn_w, peer_k1, peer_k2)
    rows = _table_rows(e_t.reshape(P_PAIRS, n).T)
    g2d = g_t.reshape(P_PAIRS, n).T
    w = _peer_u(rows, n2, g2d, u_tab)
    y2 = _peer_v(rows, w, x1, g2, v_tab, t)
    return y2.reshape(b, t, d), new_kv, s_new


def kernel(x_prompt, x_sample, cache_kv_g1, cache_kv_g2, cache_kv_g3, state_hgrn, c_prompt, c_sample, w_ada, b_ada, norm1_w, norm2_w, w_in, lb_logits, hgrn_norm_w, q_norm_w, k_norm_w, w_branch_a, w_branch_b, w_out, peer_wq, peer_qn_w, peer_k1, peer_k2, peer_u, peer_v):
    depth = w_ada.shape[0]
    bp, tp_len, _ = x_prompt.shape
    bs, ts_len, _ = x_sample.shape
    pos_p = jnp.arange(tp_len)
    pos_s = PAST_LEN + jnp.arange(ts_len)
    lb_all = jnp.cumsum(jax.nn.softmax(lb_logits.astype(F32), axis=0), axis=0)
    caches = (cache_kv_g1, cache_kv_g2, cache_kv_g3)
    c_all = jnp.concatenate([c_prompt, c_sample], axis=0)
    yp, ys = x_prompt, x_sample
    kvp, kvs = ([], [], []), ([], [], [])
    sp_list, ss_list = [], []
    for l in range(depth):
        mod = _ada(c_all, w_ada[l], b_ada[l])
        wts = (norm1_w[l], norm2_w[l], w_in[l].astype(BF16), hgrn_norm_w[l],
               jnp.stack([q_norm_w[l], k_norm_w[l]])[:, None, :],
               w_branch_a[l].astype(BF16), w_branch_b[l].astype(BF16), w_out[l].astype(BF16),
               peer_wq[l].astype(BF16), peer_qn_w[l], peer_k1[l], peer_k2[l],
               _pack_table(peer_u[l]), _pack_table(peer_v[l]))
        lb = lb_all[l].reshape(A_HEADS, A_DK)
        s0p = jnp.zeros((bp, A_HEADS, A_DK, A_DK), F32)
        yp, nkv_p, sp = _layer(yp, mod[:bp], pos_p, lb, s0p, None, wts)
        ys, nkv_s, ss = _layer(ys, mod[bp:], pos_s, lb, state_hgrn[l],
                               tuple(c[l] for c in caches), wts)
        for gi in range(N_GROUPS):
            kvp[gi].append(nkv_p[gi])
            kvs[gi].append(nkv_s[gi])
        sp_list.append(sp)
        ss_list.append(ss)
    return (yp, ys, jnp.stack(kvp[0]), jnp.stack(kvp[1]), jnp.stack(kvp[2]), jnp.stack(sp_list),
            jnp.stack(kvs[0]), jnp.stack(kvs[1]), jnp.stack(kvs[2]), jnp.stack(ss_list))
```
